```python
import math
import jax, jax.numpy as jnp
from jax import lax
import numpy as np

D_MODEL = 1024
BATCH = 8
SEQ = 4096
DEPTH = 4

N_A = DEPTH // 2
N_B = DEPTH - N_A
A_HEADS = 16
A_NOPE = 64
A_ROPE = 32
A_VDIM = 64
A_QLORA = D_MODEL // 4
A_KVLORA = D_MODEL // 8
A_WIDTH = A_HEADS * A_VDIM
A_IN = A_QLORA + A_KVLORA + A_ROPE + A_WIDTH
ROPE_THETA = 10000.0
B_HEADS = 16
B_DIM = 64
B_WIDTH = B_HEADS * B_DIM
QB = 128
EPS = 1e-6

kernel_name = 'yoco_mla_stickbreaking_hybrid'


def rmsnorm(x, g):
    xf = x.astype(jnp.float32)
    y = xf * lax.rsqrt(jnp.mean(xf * xf, axis=-1, keepdims=True) + EPS)
    return (y * g.astype(jnp.float32)).astype(x.dtype)


def rope_tables(S):
    pos = jnp.arange(S, dtype=jnp.float32)
    inv = 1.0 / (ROPE_THETA ** (jnp.arange(0, A_ROPE, 2, dtype=jnp.float32) / A_ROPE))
    ang = pos[:, None] * inv[None, :]
    return jnp.cos(ang), jnp.sin(ang)


def apply_rope(t, cos, sin):
    half = t.shape[-1] // 2
    t1, t2 = t[..., :half], t[..., half:]
    cos = cos.astype(t.dtype)
    sin = sin.astype(t.dtype)
    return jnp.concatenate([t1 * cos - t2 * sin, t1 * sin + t2 * cos], axis=-1)


def to_blocks(t):
    B, S, H, d = t.shape
    return t.reshape(B, S // QB, QB, H, d).transpose(1, 0, 3, 2, 4)


def from_blocks(o):
    nb, B, H, q, d = o.shape
    return o.transpose(1, 0, 3, 2, 4).reshape(B, nb * q, H * d)


def mla_attention(q_nope, q_rope, k_nope, k_rope, v):
    S = q_nope.shape[1]
    kpos = jnp.arange(S)
    kn = k_nope.transpose(0, 2, 1, 3)
    vv = v.transpose(0, 2, 1, 3)
    scale = 1.0 / math.sqrt(A_NOPE + A_ROPE)

    def block(args):
        qn, qr, blk = args
        qpos = blk * QB + jnp.arange(QB)
        s = (jnp.einsum('bhqd,bhkd->bhqk', qn, kn)
             + jnp.einsum('bhqr,bkr->bhqk', qr, k_rope)).astype(jnp.float32) * scale
        s = jnp.where(kpos[None, :] <= qpos[:, None], s, -jnp.inf)
        p = jax.nn.softmax(s, axis=-1)
        return jnp.einsum('bhqk,bhkd->bhqd', p.astype(vv.dtype), vv)

    o = lax.map(block, (to_blocks(q_nope), to_blocks(q_rope), jnp.arange(S // QB)))
    return from_blocks(o)


def stick_breaking_attention(q, k, v):
    S = q.shape[1]
    kpos = jnp.arange(S)
    scale = 1.0 / math.sqrt(B_DIM)

    def block(args):
        qb, blk = args
        qpos = blk * QB + jnp.arange(QB)
        mask = kpos[None, :] < qpos[:, None]
        z = jnp.einsum('bhqd,bhkd->bhqk', qb, k).astype(jnp.float32) * scale
        log_rest = jnp.where(mask, jax.nn.log_sigmoid(-z), 0.0)
        after = lax.cumsum(log_rest, axis=3, reverse=True) - log_rest
        a = jnp.where(mask, jnp.exp(jax.nn.log_sigmoid(z) + after), 0.0)
        return jnp.einsum('bhqk,bhkd->bhqd', a.astype(v.dtype), v)

    o = lax.map(block, (to_blocks(q), jnp.arange(S // QB)))
    return from_blocks(o)


def mla_mixer(h, w_in, q_norm, w_uq, kv_norm, w_ukv, cos, sin):
    B, S, _ = h.shape
    proj = h @ w_in
    c_q, c_kv, k_r, gate = jnp.split(
        proj, [A_QLORA, A_QLORA + A_KVLORA, A_QLORA + A_KVLORA + A_ROPE], axis=-1)
    q = (rmsnorm(c_q, q_norm) @ w_uq).reshape(B, S, A_HEADS, A_NOPE + A_ROPE)
    q_nope = q[..., :A_NOPE]
    q_rope = apply_rope(q[..., A_NOPE:], cos[:, None, :], sin[:, None, :])
    kv = (rmsnorm(c_kv, kv_norm) @ w_ukv).reshape(B, S, A_HEADS, A_NOPE + A_VDIM)
    k_nope, v = kv[..., :A_NOPE], kv[..., A_NOPE:]
    k_rope = apply_rope(k_r, cos, sin)
    o = mla_attention(q_nope, q_rope, k_nope, k_rope, v)
    return o * jax.nn.silu(gate)


def sb_mixer(h, w_in, k, v):
    B, S, _ = h.shape
    q, gate = jnp.split(h @ w_in, [B_WIDTH], axis=-1)
    o = stick_breaking_attention(q.reshape(B, S, B_HEADS, B_DIM), k, v)
    return o * jax.nn.silu(gate)


def setup_inputs(seed: int = 0) -> dict:
    key = jax.random.key(seed)
    ks = jax.random.split(key, 20)

    def w(k, shape, fan_in):
        return jax.random.normal(k, shape, jnp.float32) * fan_in ** -0.5

    def gain(k, shape):
        return 1.0 + 0.1 * jax.random.normal(k, shape, jnp.float32)

    return {
        'x': jax.random.normal(ks[0], (BATCH, SEQ, D_MODEL), jnp.float32),
        'a_norm_pre': gain(ks[1], (N_A, D_MODEL)),
        'a_w_in': w(ks[2], (N_A, D_MODEL, A_IN), D_MODEL),
        'a_q_norm': gain(ks[3], (N_A, A_QLORA)),
        'a_w_uq': w(ks[4], (N_A, A_QLORA, A_HEADS * (A_NOPE + A_ROPE)), A_QLORA),
        'a_kv_norm': gain(ks[5], (N_A, A_KVLORA)),
        'a_w_ukv': w(ks[6], (N_A, A_KVLORA, A_HEADS * (A_NOPE + A_VDIM)), A_KVLORA),
        'a_w_o': w(ks[7], (N_A, A_WIDTH, D_MODEL), A_WIDTH),
        'a_norm_post': gain(ks[8], (N_A, D_MODEL)),
        'b_kv_norm': gain(ks[9], (D_MODEL,)),
        'b_w_kv': w(ks[10], (D_MODEL, 2 * B_WIDTH), D_MODEL),
        'b_norm_pre': gain(ks[11], (N_B, D_MODEL)),
        'b_w_in': w(ks[12], (N_B, D_MODEL, 2 * B_WIDTH), D_MODEL),
        'b_w_o': w(ks[13], (N_B, B_WIDTH, D_MODEL), B_WIDTH),
        'b_norm_post': gain(ks[14], (N_B, D_MODEL)),
    }


def reference(x, a_norm_pre, a_w_in, a_q_norm, a_w_uq, a_kv_norm, a_w_ukv, a_w_o,
              a_norm_post, b_kv_norm, b_w_kv, b_norm_pre, b_w_in, b_w_o, b_norm_post):
    B, S, _ = x.shape
    cos, sin = rope_tables(S)
    k_shared = None
    v_shared = None
    for layer in range(DEPTH):
        if layer < N_A:
            i = layer
            h = rmsnorm(x, a_norm_pre[i])
            out = mla_mixer(h, a_w_in[i], a_q_norm[i], a_w_uq[i], a_kv_norm[i],
                            a_w_ukv[i], cos, sin) @ a_w_o[i]
            x = x + rmsnorm(out, a_norm_post[i])
        else:
            j = layer - N_A
            if j == 0:
                kv = rmsnorm(x, b_kv_norm) @ b_w_kv
                k_s, v_s = jnp.split(kv, [B_WIDTH], axis=-1)
                k_shared = k_s.reshape(B, S, B_HEADS, B_DIM).transpose(0, 2, 1, 3)
                v_shared = v_s.reshape(B, S, B_HEADS, B_DIM).transpose(0, 2, 1, 3)
            h = rmsnorm(x, b_norm_pre[j])
            out = sb_mixer(h, b_w_in[j], k_shared, v_shared) @ b_w_o[j]
            x = x + rmsnorm(out, b_norm_post[j])
    return x
```

```python
import functools
import math

import jax
import jax.numpy as jnp
from jax import lax
from jax.experimental import pallas as pl
from jax.experimental.pallas import tpu as pltpu

D_MODEL = 1024
A_HEADS = 16
A_NOPE = 64
A_ROPE = 32
A_VDIM = 64
A_QLORA = 256
A_KVLORA = 128
B_HEADS = 16
B_DIM = 64
ROPE_THETA = 10000.0
EPS = 1e-6

LANES = 128
ROW_TILE = 512
ATTN_TILE = 256
VMEM_LIMIT = 48 * 1024 * 1024

F32 = jnp.float32
BF16 = jnp.bfloat16


def _rms_scale(x):
    return lax.rsqrt(jnp.mean(x * x, axis=-1, keepdims=True) + EPS)


def _dot(a, b):
    return jnp.dot(a, b, preferred_element_type=F32)


def _dot_nt(a, b):
    return lax.dot_general(a, b, (((1,), (1,)), ((), ())), preferred_element_type=F32)


def _mla_proj_kernel(x_ref, g_ref, win_ref, qg_ref, wq_ref, kvg_ref, wkv_ref, cos_ref, sin_ref,
                     q_ref, k_ref, v_ref):
    x = x_ref[...]
    h = x * _rms_scale(x) * g_ref[...]
    proj = _dot(h.astype(BF16), win_ref[...])
    cq = proj[:, :A_QLORA]
    ckv = proj[:, A_QLORA:A_QLORA + A_KVLORA]
    kr_a = proj[:, A_QLORA + A_KVLORA:A_QLORA + A_KVLORA + LANES]
    kr_b = proj[:, A_QLORA + A_KVLORA + LANES:]
    cqn = cq * _rms_scale(cq) * qg_ref[...]
    q2 = _dot(cqn.astype(BF16), wq_ref[...])
    ckvn = ckv * _rms_scale(ckv) * kvg_ref[...]
    kv = _dot(ckvn.astype(BF16), wkv_ref[...])
    cos = cos_ref[...]
    sin = sin_ref[...]
    kr = kr_a * cos + kr_b * sin
    scale = 1.0 / math.sqrt(A_NOPE + A_ROPE)
    cos_q = cos * scale
    sin_q = sin * scale
    hw = A_HEADS * LANES
    for hd in range(A_HEADS):
        sl = slice(hd * LANES, (hd + 1) * LANES)
        sl_rot = slice(hw + hd * LANES, hw + (hd + 1) * LANES)
        q_ref[:, sl] = (q2[:, sl] * cos_q + q2[:, sl_rot] * sin_q).astype(BF16)
        k_ref[:, sl] = (kv[:, sl] + kr).astype(BF16)
    v_ref[...] = kv[:, hw:].astype(BF16)


def _sb_proj_kernel(*refs, with_kv):
    if with_kv:
        x_ref, gq_ref, wq_ref, gkv_ref, wkv_ref, q_ref, k_ref, v_ref = refs
    else:
        x_ref, gq_ref, wq_ref, q_ref = refs
    x = x_ref[...]
    y = x * _rms_scale(x)
    q = _dot((y * gq_ref[...]).astype(BF16), wq_ref[...])
    q_ref[...] = (q * (1.0 / math.sqrt(B_DIM))).astype(BF16)
    if with_kv:
        kv = _dot((y * gkv_ref[...]).astype(BF16), wkv_ref[...])
        width = B_HEADS * B_DIM
        k_ref[...] = kv[:, :width].astype(BF16)
        v_ref[...] = kv[:, width:].astype(BF16)


def _out_kernel(x_ref, o_ref, gpre_ref, wg_ref, wo_ref, gpost_ref, xn_ref):
    x = x_ref[...]
    h = x * _rms_scale(x) * gpre_ref[...]
    gate = _dot(h.astype(BF16), wg_ref[...])
    og = o_ref[...] * (gate * (1.0 / (1.0 + jnp.exp(-gate))))
    out = _dot(og.astype(BF16), wo_ref[...])
    xn_ref[...] = x + out * _rms_scale(out) * gpost_ref[...]


def _row_spec(tm, width):
    return pl.BlockSpec((tm, width), lambda i: (i, 0))


def _full_spec(shape):
    return pl.BlockSpec(shape, lambda i: (0,) * len(shape))


def _row_params():
    return pltpu.CompilerParams(dimension_semantics=("parallel",), vmem_limit_bytes=VMEM_LIMIT)


def _mla_proj(x2, g, w_in_p, qg, w_q_p, kvg, w_kv_p, cos_t, sin_t, seq):
    rows = x2.shape[0]
    tm = min(ROW_TILE, seq)
    tiles_per_seq = seq // tm
    hw = A_HEADS * LANES
    table_spec = pl.BlockSpec((tm, LANES), lambda i: (i % tiles_per_seq, 0))
    return pl.pallas_call(
        _mla_proj_kernel,
        grid=(rows // tm,),
        in_specs=[_row_spec(tm, D_MODEL), _full_spec(g.shape), _full_spec(w_in_p.shape),
                  _full_spec(qg.shape), _full_spec(w_q_p.shape), _full_spec(kvg.shape),
                  _full_spec(w_kv_p.shape), table_spec, table_spec],
        out_specs=[_row_spec(tm, hw), _row_spec(tm, hw), _row_spec(tm, A_HEADS * A_VDIM)],
        out_shape=[jax.ShapeDtypeStruct((rows, hw), BF16), jax.ShapeDtypeStruct((rows, hw), BF16),
                   jax.ShapeDtypeStruct((rows, A_HEADS * A_VDIM), BF16)],
        compiler_params=_row_params(),
        name="mla_proj",
    )(x2, g, w_in_p, qg, w_q_p, kvg, w_kv_p, cos_t, sin_t)


def _sb_proj(x2, gq, w_q, gkv=None, w_kv=None):
    rows = x2.shape[0]
    tm = min(ROW_TILE, rows)
    width = B_HEADS * B_DIM
    with_kv = w_kv is not None
    args = [x2, gq, w_q] + ([gkv, w_kv] if with_kv else [])
    in_specs = [_row_spec(tm, D_MODEL)] + [_full_spec(a.shape) for a in args[1:]]
    n_out = 3 if with_kv else 1
    return pl.pallas_call(
        functools.partial(_sb_proj_kernel, with_kv=with_kv),
        grid=(rows // tm,),
        in_specs=in_specs,
        out_specs=[_row_spec(tm, width)] * n_out,
        out_shape=[jax.ShapeDtypeStruct((rows, width), BF16)] * n_out,
        compiler_params=_row_params(),
        name="sb_proj_kv" if with_kv else "sb_proj",
    )(*args)


def _out_proj(x2, o2, gpre, w_gate, w_o, gpost):
    rows = x2.shape[0]
    tm = min(ROW_TILE, rows)
    return pl.pallas_call(
        _out_kernel,
        grid=(rows // tm,),
        in_specs=[_row_spec(tm, D_MODEL), _row_spec(tm, o2.shape[1]), _full_spec(gpre.shape),
                  _full_spec(w_gate.shape), _full_spec(w_o.shape), _full_spec(gpost.shape)],
        out_specs=_row_spec(tm, D_MODEL),
        out_shape=jax.ShapeDtypeStruct((rows, D_MODEL), F32),
        compiler_params=_row_params(),
        name="out_proj",
    )(x2, o2, gpre, w_gate, w_o, gpost)


def _mla_attn_kernel(q_ref, k_ref, v_ref, o_ref, *, tile, seq):
    lane = lax.broadcasted_iota(jnp.int32, (tile, LANES), 1)
    row = lax.broadcasted_iota(jnp.int32, (tile, tile), 0)
    col = lax.broadcasted_iota(jnp.int32, (tile, tile), 1)
    causal = col <= row

    def q_body(qi, _):
        r0 = pl.multiple_of(qi * tile, tile)
        qs = [q_ref[pl.ds(r0, tile), hd * LANES:(hd + 1) * LANES] for hd in range(2)]

        def step(j, carry, masked):
            c0 = pl.multiple_of(j * tile, tile)
            v = v_ref[pl.ds(c0, tile), :]
            new = []
            for hd in range(2):
                m, l, acc = carry[hd]
                k = k_ref[pl.ds(c0, tile), hd * LANES:(hd + 1) * LANES]
                s = _dot_nt(qs[hd], k)
                if masked:
                    s = jnp.where(causal, s, -jnp.inf)
                m_new = jnp.maximum(m, jnp.max(s, axis=-1, keepdims=True))
                alpha = jnp.exp(m - m_new)
                p = jnp.exp(s - m_new)
                l = alpha * l + jnp.sum(p, axis=-1, keepdims=True)
                acc = alpha * acc + _dot(p.astype(BF16), v)
                new.append((m_new, l, acc))
            return tuple(new)

        init = tuple((jnp.full((tile, 1), -jnp.inf, F32), jnp.zeros((tile, 1), F32),
                      jnp.zeros((tile, LANES), F32)) for _ in range(2))
        carry = lax.fori_loop(0, qi, lambda j, c: step(j, c, False), init)
        (_, l0, acc0), (_, l1, acc1) = step(qi, carry, True)
        o_ref[pl.ds(r0, tile), :] = jnp.where(lane < A_VDIM, acc0 / l0, acc1 / l1)
        return 0

    lax.fori_loop(0, seq // tile, q_body, 0)


def _sb_attn_kernel(q_ref, k_ref, v_ref, u_ref, o_ref, *, tile, seq):
    lane = lax.broadcasted_iota(jnp.int32, (tile, LANES), 1)
    row = lax.broadcasted_iota(jnp.int32, (tile, tile), 0)
    col = lax.broadcasted_iota(jnp.int32, (tile, tile), 1)
    strict = col < row
    first = lane < B_DIM

    def q_body(qi, _):
        r0 = pl.multiple_of(qi * tile, tile)
        q = q_ref[pl.ds(r0, tile), :]
        zero = jnp.zeros_like(q)
        qs = [jnp.where(first, q, zero), jnp.where(first, zero, q)]

        def step(j, carry, masked):
            c0 = pl.multiple_of(j * tile, tile)
            k = k_ref[pl.ds(c0, tile), :]
            v = v_ref[pl.ds(c0, tile), :]
            new = []
            for hd in range(2):
                later, acc = carry[hd]
                z = _dot_nt(qs[hd], k)
                lse = jnp.log(1.0 + jnp.exp(-jnp.abs(z)))
                log_rest = -(jnp.maximum(z, 0.0) + lse)
                log_beta = jnp.minimum(z, 0.0) - lse
                if masked:
                    log_rest = jnp.where(strict, log_rest, 0.0)
                hi = log_rest.astype(BF16)
                lo = (log_rest - hi.astype(F32)).astype(BF16)
                within = _dot(jnp.concatenate([hi, lo], axis=1), u_ref[...])
                a = jnp.exp(log_beta + within + later)
                if masked:
                    a = jnp.where(strict, a, 0.0)
                acc = acc + _dot(a.astype(BF16), v)
                later = later + jnp.sum(log_rest, axis=-1, keepdims=True)
                new.append((later, acc))
            return tuple(new)

        init = tuple((jnp.zeros((tile, 1), F32), jnp.zeros((tile, LANES), F32)) for _ in range(2))
        carry = step(qi, init, True)
        (_, acc0), (_, acc1) = lax.fori_loop(0, qi, lambda t, c: step(qi - 1 - t, c, False), carry)
        o_ref[pl.ds(r0, tile), :] = jnp.where(first, acc0, acc1)
        return 0

    lax.fori_loop(0, seq // tile, q_body, 0)


def _attn_params():
    return pltpu.CompilerParams(dimension_semantics=("parallel", "parallel"), vmem_limit_bytes=VMEM_LIMIT)


def _mla_attn(q, k, v):
    batch, seq, _ = q.shape
    tile = min(ATTN_TILE, seq)
    pairs = A_HEADS // 2
    return pl.pallas_call(
        functools.partial(_mla_attn_kernel, tile=tile, seq=seq),
        grid=(batch, pairs),
        in_specs=[pl.BlockSpec((None, seq, 2 * LANES), lambda b, p: (b, 0, p)),
                  pl.BlockSpec((None, seq, 2 * LANES), lambda b, p: (b, 0, p)),
                  pl.BlockSpec((None, seq, LANES), lambda b, p: (b, 0, p))],
        out_specs=pl.BlockSpec((None, seq, LANES), lambda b, p: (b, 0, p)),
        out_shape=jax.ShapeDtypeStruct((batch, seq, A_HEADS * A_VDIM), F32),
        compiler_params=_attn_params(),
        name="mla_attn",
    )(q, k, v)


def _sb_attn(q, k, v, u2):
    batch, seq, _ = q.shape
    tile = u2.shape[1]
    pairs = B_HEADS // 2
    spec = pl.BlockSpec((None, seq, LANES), lambda b, p: (b, 0, p))
    return pl.pallas_call(
        functools.partial(_sb_attn_kernel, tile=tile, seq=seq),
        grid=(batch, pairs),
        in_specs=[spec, spec, spec, pl.BlockSpec(u2.shape, lambda b, p: (0, 0))],
        out_specs=spec,
        out_shape=jax.ShapeDtypeStruct((batch, seq, B_HEADS * B_DIM), F32),
        compiler_params=_attn_params(),
        name="sb_attn",
    )(q, k, v, u2)


def _rope_tables(seq):
    pos = jnp.arange(seq, dtype=F32)
    inv = 1.0 / (ROPE_THETA ** (jnp.arange(0, A_ROPE, 2, dtype=F32) / A_ROPE))
    ang = pos[:, None] * inv[None, :]
    cos, sin = jnp.cos(ang), jnp.sin(ang)
    ones = jnp.ones((seq, A_NOPE), F32)
    zeros_n = jnp.zeros((seq, A_NOPE), F32)
    zeros_p = jnp.zeros((seq, LANES - A_NOPE - A_ROPE), F32)
    cos_t = jnp.concatenate([ones, cos, cos, zeros_p], axis=1)
    sin_t = jnp.concatenate([zeros_n, sin, sin, zeros_p], axis=1)
    return cos_t, sin_t


def _rot_half_cols(w):
    half = A_ROPE // 2
    return jnp.concatenate([-w[..., half:], w[..., :half]], axis=-1)


def _mla_weights(w_in, w_uq, w_ukv):
    d = w_in.shape[0]
    pad = LANES - A_NOPE - A_ROPE
    c_q = w_in[:, :A_QLORA]
    c_kv = w_in[:, A_QLORA:A_QLORA + A_KVLORA]
    k_r = w_in[:, A_QLORA + A_KVLORA:A_QLORA + A_KVLORA + A_ROPE]
    w_gate = w_in[:, A_QLORA + A_KVLORA + A_ROPE:]
    zn = jnp.zeros((d, A_NOPE), F32)
    zp = jnp.zeros((d, pad), F32)
    w_in_p = jnp.concatenate([c_q, c_kv, zn, k_r, zp, zn, _rot_half_cols(k_r), zp], axis=1)

    wq = w_uq.reshape(A_QLORA, A_HEADS, A_NOPE + A_ROPE)
    nope, rope = wq[..., :A_NOPE], wq[..., A_NOPE:]
    zqn = jnp.zeros((A_QLORA, A_HEADS, A_NOPE), F32)
    zqp = jnp.zeros((A_QLORA, A_HEADS, pad), F32)
    q_a = jnp.concatenate([nope, rope, zqp], axis=-1).reshape(A_QLORA, A_HEADS * LANES)
    q_b = jnp.concatenate([zqn, _rot_half_cols(rope), zqp], axis=-1).reshape(A_QLORA, A_HEADS * LANES)
    w_q_p = jnp.concatenate([q_a, q_b], axis=1)

    wkv = w_ukv.reshape(A_KVLORA, A_HEADS, A_NOPE + A_VDIM)
    kn, vv = wkv[..., :A_NOPE], wkv[..., A_NOPE:]
    zk = jnp.zeros((A_KVLORA, A_HEADS, LANES - A_NOPE), F32)
    k_p = jnp.concatenate([kn, zk], axis=-1).reshape(A_KVLORA, A_HEADS * LANES)
    w_kv_p = jnp.concatenate([k_p, vv.reshape(A_KVLORA, A_HEADS * A_VDIM)], axis=1)
    return w_in_p.astype(BF16), w_gate.astype(BF16), w_q_p.astype(BF16), w_kv_p.astype(BF16)


def _later_key_matrix(tile):
    j = lax.broadcasted_iota(jnp.int32, (tile, tile), 0)
    s = lax.broadcasted_iota(jnp.int32, (tile, tile), 1)
    u = (j > s).astype(BF16)
    return jnp.concatenate([u, u], axis=0)


def kernel(x, a_norm_pre, a_w_in, a_q_norm, a_w_uq, a_kv_norm, a_w_ukv, a_w_o, a_norm_post,
           b_kv_norm, b_w_kv, b_norm_pre, b_w_in, b_w_o, b_norm_post):
    batch, seq, d = x.shape
    rows = batch * seq
    x2 = x.reshape(rows, d)
    cos_t, sin_t = _rope_tables(seq)
    n_a = a_w_in.shape[0]
    n_b = b_w_in.shape[0]

    for i in range(n_a):
        w_in_p, w_gate, w_q_p, w_kv_p = _mla_weights(a_w_in[i], a_w_uq[i], a_w_ukv[i])
        q, k, v = _mla_proj(x2, a_norm_pre[i][None], w_in_p, a_q_norm[i][None], w_q_p,
                            a_kv_norm[i][None], w_kv_p, cos_t, sin_t, seq)
        o = _mla_attn(q.reshape(batch, seq, -1), k.reshape(batch, seq, -1), v.reshape(batch, seq, -1))
        x2 = _out_proj(x2, o.reshape(rows, -1), a_norm_pre[i][None], w_gate,
                       a_w_o[i].astype(BF16), a_norm_post[i][None])

    width = B_HEADS * B_DIM
    u2 = _later_key_matrix(min(ATTN_TILE, seq))
    k = v = None
    for j in range(n_b):
        w_q = b_w_in[j][:, :width].astype(BF16)
        w_gate = b_w_in[j][:, width:].astype(BF16)
        if j == 0:
            q, k, v = _sb_proj(x2, b_norm_pre[j][None], w_q, b_kv_norm[None], b_w_kv.astype(BF16))
            k = k.reshape(batch, seq, width)
            v = v.reshape(batch, seq, width)
        else:
            (q,) = _sb_proj(x2, b_norm_pre[j][None], w_q)
        o = _sb_attn(q.reshape(batch, seq, width), k, v, u2)
        x2 = _out_proj(x2, o.reshape(rows, width), b_norm_pre[j][None], w_gate,
                       b_w_o[j].astype(BF16), b_norm_post[j][None])
    return x2.reshape(batch, seq, d)
```

```python
import functools
import math

import jax
import jax.numpy as jnp
from jax import lax
from jax.experimental import pallas as pl
from jax.experimental.pallas import tpu as pltpu

D_MODEL = 1024
A_HEADS = 16
A_NOPE = 64
A_ROPE = 32
A_VDIM = 64
A_QLORA = 256
A_KVLORA = 128
B_HEADS = 16
B_DIM = 64
ROPE_THETA = 10000.0
EPS = 1e-6

LANES = 128
ROW_TILE = 512
ATTN_TQ = 256
ATTN_TK = 512
MLA_HEADS_PER_STEP = 4
SB_HEADS_PER_STEP = 4
SB_BLOCK = 128
CHUNK = 32
VMEM_LIMIT = 48 * 1024 * 1024
LOG2E = 1.4426950408889634

F32 = jnp.float32
BF16 = jnp.bfloat16


def _rms_scale(x):
    return lax.rsqrt(jnp.mean(x * x, axis=-1, keepdims=True) + EPS)


def _dot(a, b):
    return jnp.dot(a, b, preferred_element_type=F32)


def _dot_nt(a, b):
    return lax.dot_general(a, b, (((1,), (1,)), ((), ())), preferred_element_type=F32)


def _store_key_tiles(vt_ref, v_t):
    tile = vt_ref.shape[-1]
    for t in range(vt_ref.shape[0]):
        vt_ref[t] = v_t[:, t * tile:(t + 1) * tile].astype(BF16)


def _mla_proj_kernel(x_ref, g_ref, win_ref, qg_ref, wq_ref, kvg_ref, wk_ref, wvt_ref, cos_ref, sin_ref,
                     q_ref, k_ref, vt_ref):
    x = x_ref[...]
    h = x * _rms_scale(x) * g_ref[...]
    proj = _dot(h.astype(BF16), win_ref[...])
    cq = proj[:, :A_QLORA]
    ckv = proj[:, A_QLORA:A_QLORA + A_KVLORA]
    kr_a = proj[:, A_QLORA + A_KVLORA:A_QLORA + A_KVLORA + LANES]
    kr_b = proj[:, A_QLORA + A_KVLORA + LANES:]
    cqn = cq * _rms_scale(cq) * qg_ref[...]
    q2 = _dot(cqn.astype(BF16), wq_ref[...])
    ckvn = (ckv * _rms_scale(ckv) * kvg_ref[...]).astype(BF16)
    kn = _dot(ckvn, wk_ref[...])
    _store_key_tiles(vt_ref, _dot_nt(wvt_ref[...], ckvn))
    cos = cos_ref[...]
    sin = sin_ref[...]
    kr = kr_a * cos + kr_b * sin
    scale = LOG2E / math.sqrt(A_NOPE + A_ROPE)
    cos_q = cos * scale
    sin_q = sin * scale
    hw = A_HEADS * LANES
    for hd in range(A_HEADS):
        sl = slice(hd * LANES, (hd + 1) * LANES)
        sl_rot = slice(hw + hd * LANES, hw + (hd + 1) * LANES)
        q_ref[:, sl] = (q2[:, sl] * cos_q + q2[:, sl_rot] * sin_q).astype(BF16)
        k_ref[:, sl] = (kn[:, sl] + kr).astype(BF16)


def _sb_proj_kernel(*refs, with_kv):
    if with_kv:
        x_ref, gq_ref, wq_ref, gkv_ref, wk_ref, wvt_ref, q_ref, k_ref, vt_ref = refs
    else:
        x_ref, gq_ref, wq_ref, q_ref = refs
    x = x_ref[...]
    y = x * _rms_scale(x)
    q = _dot((y * gq_ref[...]).astype(BF16), wq_ref[...])
    q_ref[...] = (q * (LOG2E / math.sqrt(B_DIM))).astype(BF16)
    if with_kv:
        h_kv = (y * gkv_ref[...]).astype(BF16)
        k_ref[...] = _dot(h_kv, wk_ref[...]).astype(BF16)
        _store_key_tiles(vt_ref, _dot_nt(wvt_ref[...], h_kv))


def _out_kernel(x_ref, o_ref, gpre_ref, wg_ref, wo_ref, gpost_ref, xn_ref):
    x = x_ref[...]
    h = x * _rms_scale(x) * gpre_ref[...]
    gate = _dot(h.astype(BF16), wg_ref[...])
    og = o_ref[...] * (gate * (1.0 / (1.0 + jnp.exp(-gate))))
    out = _dot(og.astype(BF16), wo_ref[...])
    xn_ref[...] = x + out * _rms_scale(out) * gpost_ref[...]


def _row_spec(tm, width):
    return pl.BlockSpec((tm, width), lambda i: (i, 0))


def _full_spec(shape):
    return pl.BlockSpec(shape, lambda i: (0,) * len(shape))


def _row_params():
    return pltpu.CompilerParams(dimension_semantics=("parallel",), vmem_limit_bytes=VMEM_LIMIT)


def _vt_out(rows, seq, tm, tile, width):
    tiles_per_seq = seq // tm
    spec = pl.BlockSpec((None, tm // tile, width, tile), lambda i: (i // tiles_per_seq, i % tiles_per_seq, 0, 0))
    return spec, jax.ShapeDtypeStruct((rows // seq, seq // tile, width, tile), BF16)


def _mla_proj(x2, g, w_in_p, qg, w_q_p, kvg, w_k_p, w_vt, cos_t, sin_t, seq, tile):
    rows = x2.shape[0]
    tm = min(ROW_TILE, seq)
    tiles_per_seq = seq // tm
    hw = A_HEADS * LANES
    table_spec = pl.BlockSpec((tm, LANES), lambda i: (i % tiles_per_seq, 0))
    vt_spec, vt_shape = _vt_out(rows, seq, tm, tile, A_HEADS * A_VDIM)
    return pl.pallas_call(
        _mla_proj_kernel,
        grid=(rows // tm,),
        in_specs=[_row_spec(tm, D_MODEL), _full_spec(g.shape), _full_spec(w_in_p.shape),
                  _full_spec(qg.shape), _full_spec(w_q_p.shape), _full_spec(kvg.shape),
                  _full_spec(w_k_p.shape), _full_spec(w_vt.shape), table_spec, table_spec],
        out_specs=[_row_spec(tm, hw), _row_spec(tm, hw), vt_spec],
        out_shape=[jax.ShapeDtypeStruct((rows, hw), BF16), jax.ShapeDtypeStruct((rows, hw), BF16), vt_shape],
        compiler_params=_row_params(),
        name="mla_proj",
    )(x2, g, w_in_p, qg, w_q_p, kvg, w_k_p, w_vt, cos_t, sin_t)


def _sb_proj(x2, gq, w_q, seq, tile, gkv=None, w_k=None, w_vt=None):
    rows = x2.shape[0]
    tm = min(ROW_TILE, seq)
    width = B_HEADS * B_DIM
    with_kv = w_k is not None
    args = [x2, gq, w_q] + ([gkv, w_k, w_vt] if with_kv else [])
    in_specs = [_row_spec(tm, D_MODEL)] + [_full_spec(a.shape) for a in args[1:]]
    out_specs = [_row_spec(tm, width)]
    out_shape = [jax.ShapeDtypeStruct((rows, width), BF16)]
    if with_kv:
        vt_spec, vt_shape = _vt_out(rows, seq, tm, tile, width)
        out_specs += [_row_spec(tm, width), vt_spec]
        out_shape += [jax.ShapeDtypeStruct((rows, width), BF16), vt_shape]
    return pl.pallas_call(
        functools.partial(_sb_proj_kernel, with_kv=with_kv),
        grid=(rows // tm,),
        in_specs=in_specs,
        out_specs=out_specs,
        out_shape=out_shape,
        compiler_params=_row_params(),
        name="sb_proj_kv" if with_kv else "sb_proj",
    )(*args)


def _out_proj(x2, o2, gpre, w_gate, w_o, gpost):
    rows = x2.shape[0]
    tm = min(ROW_TILE, rows)
    return pl.pallas_call(
        _out_kernel,
        grid=(rows // tm,),
        in_specs=[_row_spec(tm, D_MODEL), _row_spec(tm, o2.shape[1]), _full_spec(gpre.shape),
                  _full_spec(w_gate.shape), _full_spec(w_o.shape), _full_spec(gpost.shape)],
        out_specs=_row_spec(tm, D_MODEL),
        out_shape=jax.ShapeDtypeStruct((rows, D_MODEL), F32),
        compiler_params=_row_params(),
        name="out_proj",
    )(x2, o2, gpre, w_gate, w_o, gpost)


def _mla_attn_kernel(q_ref, k_ref, vt_ref, o_ref, s_scr0, s_scr1, *, tq, tk, seq, heads):
    s_scr = (s_scr0, s_scr1)
    key = lax.broadcasted_iota(jnp.int32, (CHUNK, tq), 0)
    qry = lax.broadcasted_iota(jnp.int32, (CHUNK, tq), 1)

    def q_body(qi, _):
        r0 = pl.multiple_of(qi * tq, tq)
        qs = [q_ref[pl.ds(r0, tq), hd * LANES:(hd + 1) * LANES] for hd in range(heads)]
        diag = (qi * tq) // tk

        def scores(j, slot):
            c0 = pl.multiple_of(j * tk, tk)
            tile_max = []
            for hd in range(heads):
                s = _dot_nt(k_ref[pl.ds(c0, tk), hd * LANES:(hd + 1) * LANES], qs[hd])
                s_scr[slot][hd] = s
                tile_max.append(jnp.max(s, axis=0, keepdims=True))
            return tile_max

        def score_chunk(slot, hd, c, causal_limit):
            blk = s_scr[slot][hd, c * CHUNK:(c + 1) * CHUNK, :]
            if causal_limit is None:
                return blk
            return jnp.where(key + c * CHUNK <= causal_limit, blk, -jnp.inf)

        def tile(j, slot, carry, tile_max, causal_limit=None):
            new = []
            for hd, (m, l, acc) in enumerate(carry):
                if causal_limit is None:
                    mx = tile_max[hd]
                else:
                    mx = score_chunk(slot, hd, 0, causal_limit)
                    for c in range(1, tk // CHUNK):
                        mx = jnp.maximum(mx, score_chunk(slot, hd, c, causal_limit))
                    mx = jnp.max(mx, axis=0, keepdims=True)
                m_new = jnp.maximum(m, mx)
                alpha = jnp.exp2(m - m_new)
                total, ps = None, []
                for c in range(tk // CHUNK):
                    p = jnp.exp2(score_chunk(slot, hd, c, causal_limit) - m_new)
                    ps.append(p.astype(BF16))
                    total = p if total is None else total + p
                l = alpha * l + jnp.sum(total, axis=0, keepdims=True)
                vt = vt_ref[j, hd * A_VDIM:(hd + 1) * A_VDIM, :]
                acc = alpha * acc + _dot(vt, jnp.concatenate(ps, axis=0))
                new.append((m_new, l, acc))
            return tuple(new)

        def step(j, state, slot):
            carry, tile_max = state
            next_max = scores(j + 1, 1 - slot)
            return tile(j, slot, carry, tile_max), next_max

        def body(j, state):
            return lax.cond(j % 2 == 0, functools.partial(step, slot=0), functools.partial(step, slot=1),
                            j, state)

        def finish(carry, slot):
            carry = tile(diag, slot, carry, None, qry + (r0 - diag * tk))
            outs = [acc / l for _, l, acc in carry]
            o_ref[pl.ds(r0, tq), :] = jnp.concatenate(outs, axis=0).T

        first_max = scores(0, 0)
        init = tuple((jnp.full((1, tq), -jnp.inf, F32), jnp.zeros((1, tq), F32),
                      jnp.zeros((A_VDIM, tq), F32)) for _ in range(heads))
        carry, _ = lax.fori_loop(0, diag, body, (init, first_max))

        @pl.when(diag % 2 == 0)
        def _():
            finish(carry, 0)

        @pl.when(diag % 2 == 1)
        def _():
            finish(carry, 1)

        return 0

    lax.fori_loop(0, seq // tq, q_body, 0)


def _sb_attn_kernel(q_ref, k_ref, vt_ref, later_ref, o_ref, z_scr0, z_scr1, *, tq, tk, seq, heads):
    blk = later_ref.shape[0]
    z_scr = (z_scr0, z_scr1)
    lane = lax.broadcasted_iota(jnp.int32, (tq, LANES), 1)
    key = lax.broadcasted_iota(jnp.int32, (CHUNK, tq), 0)
    qry = lax.broadcasted_iota(jnp.int32, (CHUNK, tq), 1)

    def q_body(qi, _):
        r0 = pl.multiple_of(qi * tq, tq)
        qs = []
        for hd in range(heads):
            pair = q_ref[pl.ds(r0, tq), (hd // 2) * LANES:(hd // 2 + 1) * LANES]
            mine = (lane < B_DIM) if hd % 2 == 0 else (lane >= B_DIM)
            qs.append(jnp.where(mine, pair, jnp.zeros_like(pair)))
        diag = (qi * tq) // tk

        def logits(j, slot):
            c0 = pl.multiple_of(j * tk, tk)
            for hd in range(heads):
                k = k_ref[pl.ds(c0, tk), (hd // 2) * LANES:(hd // 2 + 1) * LANES]
                z_scr[slot][hd] = _dot_nt(k, qs[hd])

        def block(j, slot, hd, b, later, acc, strict_limit):
            his, los, total = [], [], None
            for c in range(blk // CHUNK):
                rows = slice(b * blk + c * CHUNK, b * blk + (c + 1) * CHUNK)
                z = z_scr[slot][hd, rows, :]
                neg_lse = jnp.log(1.0 + jnp.exp2(-jnp.abs(z))) * (-LOG2E)
                log_rest = neg_lse - jnp.maximum(z, 0.0)
                z_scr[slot][hd, rows, :] = log_rest + z
                if strict_limit is not None:
                    log_rest = jnp.where(key + (b * blk + c * CHUNK) < strict_limit, log_rest, 0.0)
                hi = log_rest.astype(BF16)
                his.append(hi)
                los.append((log_rest - hi.astype(F32)).astype(BF16))
                total = log_rest if total is None else total + log_rest
            within = _dot(later_ref[...], jnp.concatenate(his + los, axis=0))
            weights = []
            for c in range(blk // CHUNK):
                rows = slice(b * blk + c * CHUNK, b * blk + (c + 1) * CHUNK)
                a = jnp.exp2(z_scr[slot][hd, rows, :] + within[c * CHUNK:(c + 1) * CHUNK, :] + later)
                if strict_limit is not None:
                    a = jnp.where(key + (b * blk + c * CHUNK) < strict_limit, a, 0.0)
                weights.append(a.astype(BF16))
            vt = vt_ref[j, hd * B_DIM:(hd + 1) * B_DIM, b * blk:(b + 1) * blk]
            acc = acc + _dot(vt, jnp.concatenate(weights, axis=0))
            return later + jnp.sum(total, axis=0, keepdims=True), acc

        def tile(j, slot, carry, strict_limit=None):
            new = []
            for hd, (later, acc) in enumerate(carry):
                for b in reversed(range(tk // blk)):
                    later, acc = block(j, slot, hd, b, later, acc, strict_limit)
                new.append((later, acc))
            return tuple(new)

        def step(n, carry, slot):
            j = diag - n
            logits(jnp.maximum(j - 1, 0), 1 - slot)
            return tile(j, slot, carry)

        def body(n, carry):
            return lax.cond(n % 2 == 0, functools.partial(step, slot=0), functools.partial(step, slot=1),
                            n, carry)

        logits(diag, 0)
        logits(jnp.maximum(diag - 1, 0), 1)
        init = tuple((jnp.zeros((1, tq), F32), jnp.zeros((B_DIM, tq), F32)) for _ in range(heads))
        carry = tile(diag, 0, init, qry + (r0 - diag * tk))
        carry = lax.fori_loop(1, diag + 1, body, carry)
        o_ref[pl.ds(r0, tq), :] = jnp.concatenate([acc for _, acc in carry], axis=0).T
        return 0

    lax.fori_loop(0, seq // tq, q_body, 0)


def _attn_params():
    return pltpu.CompilerParams(dimension_semantics=("parallel", "parallel"), vmem_limit_bytes=VMEM_LIMIT)


def _mla_attn(q, k, vt, tq):
    batch, seq, _ = q.shape
    tk = vt.shape[-1]
    heads = MLA_HEADS_PER_STEP
    qk_spec = pl.BlockSpec((None, seq, heads * LANES), lambda b, g: (b, 0, g))
    return pl.pallas_call(
        functools.partial(_mla_attn_kernel, tq=tq, tk=tk, seq=seq, heads=heads),
        grid=(batch, A_HEADS // heads),
        in_specs=[qk_spec, qk_spec,
                  pl.BlockSpec((None, seq // tk, heads * A_VDIM, tk), lambda b, g: (b, 0, g, 0))],
        out_specs=pl.BlockSpec((None, seq, heads * A_VDIM), lambda b, g: (b, 0, g)),
        out_shape=jax.ShapeDtypeStruct((batch, seq, A_HEADS * A_VDIM), F32),
        scratch_shapes=[pltpu.VMEM((heads, tk, tq), F32)] * 2,
        compiler_params=_attn_params(),
        name="mla_attn",
    )(q, k, vt)


def _sb_attn(q, k, vt, later2, tq):
    batch, seq, _ = q.shape
    tk = vt.shape[-1]
    heads = SB_HEADS_PER_STEP
    spec = pl.BlockSpec((None, seq, heads * B_DIM), lambda b, g: (b, 0, g))
    return pl.pallas_call(
        functools.partial(_sb_attn_kernel, tq=tq, tk=tk, seq=seq, heads=heads),
        grid=(batch, B_HEADS // heads),
        in_specs=[spec, spec, pl.BlockSpec((None, seq // tk, heads * B_DIM, tk), lambda b, g: (b, 0, g, 0)),
                  pl.BlockSpec(later2.shape, lambda b, g: (0, 0))],
        out_specs=spec,
        out_shape=jax.ShapeDtypeStruct((batch, seq, B_HEADS * B_DIM), F32),
        scratch_shapes=[pltpu.VMEM((heads, tk, tq), F32)] * 2,
        compiler_params=_attn_params(),
        name="sb_attn",
    )(q, k, vt, later2)


def _rope_tables(seq):
    pos = jnp.arange(seq, dtype=F32)
    inv = 1.0 / (ROPE_THETA ** (jnp.arange(0, A_ROPE, 2, dtype=F32) / A_ROPE))
    ang = pos[:, None] * inv[None, :]
    cos, sin = jnp.cos(ang), jnp.sin(ang)
    ones = jnp.ones((seq, A_NOPE), F32)
    zeros_n = jnp.zeros((seq, A_NOPE), F32)
    zeros_p = jnp.zeros((seq, LANES - A_NOPE - A_ROPE), F32)
    cos_t = jnp.concatenate([ones, cos, cos, zeros_p], axis=1)
    sin_t = jnp.concatenate([zeros_n, sin, sin, zeros_p], axis=1)
    return cos_t, sin_t


def _rot_half_cols(w):
    half = A_ROPE // 2
    return jnp.concatenate([-w[..., half:], w[..., :half]], axis=-1)


def _mla_weights(w_in, w_uq, w_ukv):
    d = w_in.shape[0]
    pad = LANES - A_NOPE - A_ROPE
    c_q = w_in[:, :A_QLORA]
    c_kv = w_in[:, A_QLORA:A_QLORA + A_KVLORA]
    k_r = w_in[:, A_QLORA + A_KVLORA:A_QLORA + A_KVLORA + A_ROPE]
    w_gate = w_in[:, A_QLORA + A_KVLORA + A_ROPE:]
    zn = jnp.zeros((d, A_NOPE), F32)
    zp = jnp.zeros((d, pad), F32)
    w_in_p = jnp.concatenate([c_q, c_kv, zn, k_r, zp, zn, _rot_half_cols(k_r), zp], axis=1)

    wq = w_uq.reshape(A_QLORA, A_HEADS, A_NOPE + A_ROPE)
    nope, rope = wq[..., :A_NOPE], wq[..., A_NOPE:]
    zqn = jnp.zeros((A_QLORA, A_HEADS, A_NOPE), F32)
    zqp = jnp.zeros((A_QLORA, A_HEADS, pad), F32)
    q_a = jnp.concatenate([nope, rope, zqp], axis=-1).reshape(A_QLORA, A_HEADS * LANES)
    q_b = jnp.concatenate([zqn, _rot_half_cols(rope), zqp], axis=-1).reshape(A_QLORA, A_HEADS * LANES)
    w_q_p = jnp.concatenate([q_a, q_b], axis=1)

    wkv = w_ukv.reshape(A_KVLORA, A_HEADS, A_NOPE + A_VDIM)
    kn, vv = wkv[..., :A_NOPE], wkv[..., A_NOPE:]
    zk = jnp.zeros((A_KVLORA, A_HEADS, LANES - A_NOPE), F32)
    w_k_p = jnp.concatenate([kn, zk], axis=-1).reshape(A_KVLORA, A_HEADS * LANES)
    w_vt = vv.reshape(A_KVLORA, A_HEADS * A_VDIM).T
    return (w_in_p.astype(BF16), w_gate.astype(BF16), w_q_p.astype(BF16), w_k_p.astype(BF16),
            w_vt.astype(BF16))


def _later_key_matrix(blk):
    s = lax.broadcasted_iota(jnp.int32, (blk, blk), 0)
    j = lax.broadcasted_iota(jnp.int32, (blk, blk), 1)
    m = (j > s).astype(BF16)
    return jnp.concatenate([m, m], axis=1)


def kernel(x, a_norm_pre, a_w_in, a_q_norm, a_w_uq, a_kv_norm, a_w_ukv, a_w_o, a_norm_post,
           b_kv_norm, b_w_kv, b_norm_pre, b_w_in, b_w_o, b_norm_post):
    batch, seq, d = x.shape
    rows = batch * seq
    tq = min(ATTN_TQ, seq)
    tk = min(ATTN_TK, seq)
    x2 = x.reshape(rows, d)
    cos_t, sin_t = _rope_tables(seq)
    n_a = a_w_in.shape[0]
    n_b = b_w_in.shape[0]

    for i in range(n_a):
        w_in_p, w_gate, w_q_p, w_k_p, w_vt = _mla_weights(a_w_in[i], a_w_uq[i], a_w_ukv[i])
        q, k, vt = _mla_proj(x2, a_norm_pre[i][None], w_in_p, a_q_norm[i][None], w_q_p,
                             a_kv_norm[i][None], w_k_p, w_vt, cos_t, sin_t, seq, tk)
        o = _mla_attn(q.reshape(batch, seq, -1), k.reshape(batch, seq, -1), vt, tq)
        x2 = _out_proj(x2, o.reshape(rows, -1), a_norm_pre[i][None], w_gate,
                       a_w_o[i].astype(BF16), a_norm_post[i][None])

    width = B_HEADS * B_DIM
    later2 = _later_key_matrix(min(SB_BLOCK, tk))
    k = vt = None
    for j in range(n_b):
        w_q = b_w_in[j][:, :width].astype(BF16)
        w_gate = b_w_in[j][:, width:].astype(BF16)
        if j == 0:
            q, k, vt = _sb_proj(x2, b_norm_pre[j][None], w_q, seq, tk, b_kv_norm[None],
                                b_w_kv[:, :width].astype(BF16), b_w_kv[:, width:].T.astype(BF16))
            k = k.reshape(batch, seq, width)
        else:
            (q,) = _sb_proj(x2, b_norm_pre[j][None], w_q, seq, tk)
        o = _sb_attn(q.reshape(batch, seq, width), k, vt, later2, tq)
        x2 = _out_proj(x2, o.reshape(rows, width), b_norm_pre[j][None], w_gate,
                       b_w_o[j].astype(BF16), b_norm_post[j][None])
    return x2.reshape(batch, seq, d)
```

```python
import functools
import math

import jax
import jax.numpy as jnp
from jax import lax
from jax.experimental import pallas as pl
from jax.experimental.pallas import tpu as pltpu

D_MODEL = 1024
A_HEADS = 16
A_NOPE = 64
A_ROPE = 32
A_VDIM = 64
A_QLORA = 256
A_KVLORA = 128
B_HEADS = 16
B_DIM = 64
ROPE_THETA = 10000.0
EPS = 1e-6

LANES = 128
ROW_TILE = 512
BF16_SUBLANES = 16
ATTN_TQ = 256
MLA_TK = 512
MLA_HEADS_PER_STEP = 4
SB_TK = 512
SB_HEADS_PER_STEP = 4
SB_BLOCK = 128
SIGN_BIT = -2147483648
F32_UNDERFLOW_LOG2 = -152.0
CHUNK = 32
VMEM_LIMIT = 56 * 1024 * 1024
LOG2E = 1.4426950408889634

F32 = jnp.float32
BF16 = jnp.bfloat16


def _rms_scale(x):
    return lax.rsqrt(jnp.mean(x * x, axis=-1, keepdims=True) + EPS)


def _dot(a, b):
    return jnp.dot(a, b, preferred_element_type=F32)


def _dot_nt(a, b):
    return lax.dot_general(a, b, (((1,), (1,)), ((), ())), preferred_element_type=F32)


def _store_key_tiles(vt_ref, v_t):
    tile = vt_ref.shape[-1]
    for t in range(vt_ref.shape[0]):
        vt_ref[t] = v_t[:, t * tile:(t + 1) * tile].astype(BF16)


def _mla_proj_kernel(x_ref, g_ref, win_ref, qg_ref, wq_ref, kvg_ref, wk_ref, wvt_ref, cos_ref, sin_ref,
                     q_ref, k_ref, vt_ref):
    x = x_ref[...]
    h = x * _rms_scale(x) * g_ref[...]
    proj = _dot(h.astype(BF16), win_ref[...])
    cq = proj[:, :A_QLORA]
    ckv = proj[:, A_QLORA:A_QLORA + A_KVLORA]
    kr_a = proj[:, A_QLORA + A_KVLORA:A_QLORA + A_KVLORA + LANES]
    kr_b = proj[:, A_QLORA + A_KVLORA + LANES:]
    cqn = cq * _rms_scale(cq) * qg_ref[...]
    q2 = _dot(cqn.astype(BF16), wq_ref[...])
    ckvn = (ckv * _rms_scale(ckv) * kvg_ref[...]).astype(BF16)
    kn = _dot(ckvn, wk_ref[...])
    _store_key_tiles(vt_ref, _dot_nt(wvt_ref[...], ckvn))
    cos = cos_ref[...]
    sin = sin_ref[...]
    kr = kr_a * cos + kr_b * sin
    scale = LOG2E / math.sqrt(A_NOPE + A_ROPE)
    cos_q = cos * scale
    sin_q = sin * scale
    hw = A_HEADS * LANES
    for hd in range(A_HEADS):
        sl = slice(hd * LANES, (hd + 1) * LANES)
        sl_rot = slice(hw + hd * LANES, hw + (hd + 1) * LANES)
        q_ref[:, sl] = (q2[:, sl] * cos_q + q2[:, sl_rot] * sin_q).astype(BF16)
        k_ref[:, sl] = (kn[:, sl] + kr).astype(BF16)


def _sb_proj_kernel(*refs, with_kv):
    if with_kv:
        x_ref, gq_ref, wq_ref, gkv_ref, wk_ref, wvt_ref, q_ref, k_ref, vt_ref = refs
    else:
        x_ref, gq_ref, wq_ref, q_ref = refs
    x = x_ref[...]
    y = x * _rms_scale(x)
    q = _dot((y * gq_ref[...]).astype(BF16), wq_ref[...])
    q_ref[...] = (q * (LOG2E / math.sqrt(B_DIM))).astype(BF16)
    if with_kv:
        h_kv = (y * gkv_ref[...]).astype(BF16)
        k_ref[...] = _dot(h_kv, wk_ref[...]).astype(BF16)
        _store_key_tiles(vt_ref, _dot_nt(wvt_ref[...], h_kv))


def _out_kernel(x_ref, o_ref, gpre_ref, wg_ref, wo_ref, gpost_ref, xn_ref):
    x = x_ref[...]
    h = x * _rms_scale(x) * gpre_ref[...]
    gate = _dot(h.astype(BF16), wg_ref[...])
    og = o_ref[...] * (gate * (1.0 / (1.0 + jnp.exp(-gate))))
    out = _dot(og.astype(BF16), wo_ref[...])
    xn_ref[...] = x + out * _rms_scale(out) * gpost_ref[...]


def _row_spec(tm, width):
    return pl.BlockSpec((tm, width), lambda i: (i, 0))


def _full_spec(shape):
    return pl.BlockSpec(shape, lambda i: (0,) * len(shape))


def _row_params():
    return pltpu.CompilerParams(dimension_semantics=("parallel",), vmem_limit_bytes=VMEM_LIMIT)


def _vt_out(rows, seq, tm, tile, width):
    tiles_per_seq = seq // tm
    spec = pl.BlockSpec((None, tm // tile, width, tile), lambda i: (i // tiles_per_seq, i % tiles_per_seq, 0, 0))
    return spec, jax.ShapeDtypeStruct((rows // seq, seq // tile, width, tile), BF16)


def _mla_proj(x2, g, w_in_p, qg, w_q_p, kvg, w_k_p, w_vt, cos_t, sin_t, seq, tile):
    rows = x2.shape[0]
    tm = min(ROW_TILE, seq)
    tiles_per_seq = seq // tm
    hw = A_HEADS * LANES
    table_spec = pl.BlockSpec((tm, LANES), lambda i: (i % tiles_per_seq, 0))
    vt_spec, vt_shape = _vt_out(rows, seq, tm, tile, A_HEADS * A_VDIM)
    return pl.pallas_call(
        _mla_proj_kernel,
        grid=(rows // tm,),
        in_specs=[_row_spec(tm, D_MODEL), _full_spec(g.shape), _full_spec(w_in_p.shape),
                  _full_spec(qg.shape), _full_spec(w_q_p.shape), _full_spec(kvg.shape),
                  _full_spec(w_k_p.shape), _full_spec(w_vt.shape), table_spec, table_spec],
        out_specs=[_row_spec(tm, hw), _row_spec(tm, hw), vt_spec],
        out_shape=[jax.ShapeDtypeStruct((rows, hw), BF16), jax.ShapeDtypeStruct((rows, hw), BF16), vt_shape],
        compiler_params=_row_params(),
        name="mla_proj",
    )(x2, g, w_in_p, qg, w_q_p, kvg, w_k_p, w_vt, cos_t, sin_t)


def _sb_proj(x2, gq, w_q, seq, tile, gkv=None, w_k=None, w_vt=None):
    rows = x2.shape[0]
    tm = min(ROW_TILE, seq)
    width = B_HEADS * B_DIM
    with_kv = w_k is not None
    args = [x2, gq, w_q] + ([gkv, w_k, w_vt] if with_kv else [])
    in_specs = [_row_spec(tm, D_MODEL)] + [_full_spec(a.shape) for a in args[1:]]
    out_specs = [_row_spec(tm, width)]
    out_shape = [jax.ShapeDtypeStruct((rows, width), BF16)]
    if with_kv:
        vt_spec, vt_shape = _vt_out(rows, seq, tm, tile, width)
        out_specs += [_row_spec(tm, width), vt_spec]
        out_shape += [jax.ShapeDtypeStruct((rows, width), BF16), vt_shape]
    return pl.pallas_call(
        functools.partial(_sb_proj_kernel, with_kv=with_kv),
        grid=(rows // tm,),
        in_specs=in_specs,
        out_specs=out_specs,
        out_shape=out_shape,
        compiler_params=_row_params(),
        name="sb_proj_kv" if with_kv else "sb_proj",
    )(*args)


def _out_proj(x2, o2, gpre, w_gate, w_o, gpost):
    rows = x2.shape[0]
    tm = min(ROW_TILE, rows)
    return pl.pallas_call(
        _out_kernel,
        grid=(rows // tm,),
        in_specs=[_row_spec(tm, D_MODEL), _row_spec(tm, o2.shape[1]), _full_spec(gpre.shape),
                  _full_spec(w_gate.shape), _full_spec(w_o.shape), _full_spec(gpost.shape)],
        out_specs=_row_spec(tm, D_MODEL),
        out_shape=jax.ShapeDtypeStruct((rows, D_MODEL), F32),
        compiler_params=_row_params(),
        name="out_proj",
    )(x2, o2, gpre, w_gate, w_o, gpost)


def _mla_attn_kernel(q_ref, k_ref, vt_ref, o_ref, s_scr0, s_scr1, *, tq, tk, seq, heads):
    s_scr = (s_scr0, s_scr1)
    key = lax.broadcasted_iota(jnp.int32, (CHUNK, tq), 0)
    qry = lax.broadcasted_iota(jnp.int32, (CHUNK, tq), 1)

    def q_body(qi, _):
        r0 = pl.multiple_of(qi * tq, tq)
        qs = [q_ref[pl.ds(r0, tq), hd * LANES:(hd + 1) * LANES] for hd in range(heads)]
        diag = (qi * tq) // tk

        def scores(j, slot):
            c0 = pl.multiple_of(j * tk, tk)
            tile_max = []
            for hd in range(heads):
                s = _dot_nt(k_ref[pl.ds(c0, tk), hd * LANES:(hd + 1) * LANES], qs[hd])
                s_scr[slot][hd] = s
                tile_max.append(jnp.max(s, axis=0, keepdims=True))
            return tile_max

        def score_chunk(slot, hd, c, causal_limit):
            blk = s_scr[slot][hd, c * CHUNK:(c + 1) * CHUNK, :]
            if causal_limit is None:
                return blk
            return jnp.where(key + c * CHUNK <= causal_limit, blk, -jnp.inf)

        def tile(j, slot, carry, tile_max, causal_limit=None):
            new = []
            for hd, (m, l, acc) in enumerate(carry):
                if causal_limit is None:
                    mx = tile_max[hd]
                else:
                    mx = score_chunk(slot, hd, 0, causal_limit)
                    for c in range(1, tk // CHUNK):
                        mx = jnp.maximum(mx, score_chunk(slot, hd, c, causal_limit))
                    mx = jnp.max(mx, axis=0, keepdims=True)
                m_new = jnp.maximum(m, mx)
                alpha = jnp.exp2(m - m_new)
                total, ps = None, []
                for c in range(tk // CHUNK):
                    p = jnp.exp2(score_chunk(slot, hd, c, causal_limit) - m_new)
                    ps.append(p.astype(BF16))
                    total = p if total is None else total + p
                l = alpha * l + jnp.sum(total, axis=0, keepdims=True)
                vt = vt_ref[j, hd * A_VDIM:(hd + 1) * A_VDIM, :]
                acc = alpha * acc + _dot(vt, jnp.concatenate(ps, axis=0))
                new.append((m_new, l, acc))
            return tuple(new)

        def step(j, state, slot):
            carry, tile_max = state
            next_max = scores(j + 1, 1 - slot)
            return tile(j, slot, carry, tile_max), next_max

        def body(j, state):
            return lax.cond(j % 2 == 0, functools.partial(step, slot=0), functools.partial(step, slot=1),
                            j, state)

        def finish(carry, slot):
            carry = tile(diag, slot, carry, None, qry + (r0 - diag * tk))
            outs = [acc / l for _, l, acc in carry]
            o_ref[pl.ds(r0, tq), :] = jnp.concatenate(outs, axis=0).T

        first_max = scores(0, 0)
        init = tuple((jnp.full((1, tq), -jnp.inf, F32), jnp.zeros((1, tq), F32),
                      jnp.zeros((A_VDIM, tq), F32)) for _ in range(heads))
        carry, _ = lax.fori_loop(0, diag, body, (init, first_max))

        @pl.when(diag % 2 == 0)
        def _():
            finish(carry, 0)

        @pl.when(diag % 2 == 1)
        def _():
            finish(carry, 1)

        return 0

    lax.fori_loop(0, seq // tq, q_body, 0)


def _sb_attn_kernel(q_ref, k_ref, vt_ref, later_ref, o_ref, z_scr0, z_scr1, *, tq, tk, seq, heads):
    blk = later_ref.shape[1] // 2
    z_scr = (z_scr0, z_scr1)
    lane = lax.broadcasted_iota(jnp.int32, (tq, LANES), 1)
    key = lax.broadcasted_iota(jnp.int32, (CHUNK, tq), 0)
    qry = lax.broadcasted_iota(jnp.int32, (CHUNK, tq), 1)

    def q_body(qi, _):
        r0 = pl.multiple_of(qi * tq, tq)
        qs = []
        for hd in range(heads):
            pair = q_ref[pl.ds(r0, tq), (hd // 2) * LANES:(hd // 2 + 1) * LANES]
            mine = (lane < B_DIM) if hd % 2 == 0 else (lane >= B_DIM)
            qs.append(jnp.where(mine, pair, jnp.zeros_like(pair)))
        diag = (qi * tq) // tk

        def logits(j, slot):
            c0 = pl.multiple_of(j * tk, tk)
            for hd in range(heads):
                k = k_ref[pl.ds(c0, tk), (hd // 2) * LANES:(hd // 2 + 1) * LANES]
                z_scr[slot][hd] = _dot_nt(k, qs[hd])

        def block(j, slot, hd, b, later, acc, strict_limit):
            his, los = [], []
            for c in range(blk // CHUNK):
                rows = slice(b * blk + c * CHUNK, b * blk + (c + 1) * CHUNK)
                z = z_scr[slot][hd, rows, :]
                neg_abs = pltpu.bitcast(pltpu.bitcast(z, jnp.int32) | SIGN_BIT, F32)
                neg_lse = jnp.log(1.0 + jnp.exp2(neg_abs)) * (-LOG2E)
                log_rest = neg_lse - jnp.maximum(z, 0.0)
                z_scr[slot][hd, rows, :] = log_rest + z
                if strict_limit is not None:
                    log_rest = jnp.where(key + (b * blk + c * CHUNK) < strict_limit, log_rest, 0.0)
                hi = log_rest.astype(BF16)
                his.append(hi)
                los.append((log_rest - hi.astype(F32)).astype(BF16))
            sums = _dot(later_ref[...], jnp.concatenate(his + los, axis=0))
            within = sums[:blk]
            block_total = sums[blk:blk + 1]
            weights = []
            for c in range(blk // CHUNK):
                rows = slice(b * blk + c * CHUNK, b * blk + (c + 1) * CHUNK)
                a = jnp.exp2(z_scr[slot][hd, rows, :] + within[c * CHUNK:(c + 1) * CHUNK, :] + later)
                if strict_limit is not None:
                    a = jnp.where(key + (b * blk + c * CHUNK) < strict_limit, a, 0.0)
                weights.append(a.astype(BF16))
            vt = vt_ref[j, hd * B_DIM:(hd + 1) * B_DIM, b * blk:(b + 1) * blk]
            acc = acc + _dot(vt, jnp.concatenate(weights, axis=0))
            return later + block_total, acc

        def tile(j, slot, carry, strict_limit=None):
            new = []
            for hd, (later, acc) in enumerate(carry):
                for b in reversed(range(tk // blk)):
                    later, acc = block(j, slot, hd, b, later, acc, strict_limit)
                new.append((later, acc))
            return tuple(new)

        def step(n, carry, slot):
            j = diag - n
            logits(jnp.maximum(j - 1, 0), 1 - slot)
            return tile(j, slot, carry)

        def any_weight_left(carry):
            worst = carry[0][0]
            for later, _ in carry[1:]:
                worst = jnp.maximum(worst, later)
            return (jnp.max(worst) > F32_UNDERFLOW_LOG2).astype(jnp.int32)

        def more(state):
            n, live, _ = state
            return jnp.logical_and(n <= diag, live > 0)

        def body(state):
            n, _, carry = state
            carry = lax.cond(n % 2 == 0, functools.partial(step, slot=0), functools.partial(step, slot=1),
                             n, carry)
            return n + 1, any_weight_left(carry), carry

        logits(diag, 0)
        logits(jnp.maximum(diag - 1, 0), 1)
        init = tuple((jnp.zeros((1, tq), F32), jnp.zeros((B_DIM, tq), F32)) for _ in range(heads))
        carry = tile(diag, 0, init, qry + (r0 - diag * tk))
        _, _, carry = lax.while_loop(more, body, (jnp.int32(1), any_weight_left(carry), carry))
        o_ref[pl.ds(r0, tq), :] = jnp.concatenate([acc for _, acc in carry], axis=0).T
        return 0

    lax.fori_loop(0, seq // tq, q_body, 0)


def _attn_params():
    return pltpu.CompilerParams(dimension_semantics=("parallel", "parallel"), vmem_limit_bytes=VMEM_LIMIT)


def _mla_attn(q, k, vt, tq):
    batch, seq, _ = q.shape
    tk = vt.shape[-1]
    heads = MLA_HEADS_PER_STEP
    qk_spec = pl.BlockSpec((None, seq, heads * LANES), lambda b, g: (b, 0, g))
    return pl.pallas_call(
        functools.partial(_mla_attn_kernel, tq=tq, tk=tk, seq=seq, heads=heads),
        grid=(batch, A_HEADS // heads),
        in_specs=[qk_spec, qk_spec,
                  pl.BlockSpec((None, seq // tk, heads * A_VDIM, tk), lambda b, g: (b, 0, g, 0))],
        out_specs=pl.BlockSpec((None, seq, heads * A_VDIM), lambda b, g: (b, 0, g)),
        out_shape=jax.ShapeDtypeStruct((batch, seq, A_HEADS * A_VDIM), F32),
        scratch_shapes=[pltpu.VMEM((heads, tk, tq), F32)] * 2,
        compiler_params=_attn_params(),
        name="mla_attn",
    )(q, k, vt)


def _sb_attn(q, k, vt, later2, tq):
    batch, seq, _ = q.shape
    tk = vt.shape[-1]
    heads = SB_HEADS_PER_STEP
    spec = pl.BlockSpec((None, seq, heads * B_DIM), lambda b, g: (b, 0, g))
    return pl.pallas_call(
        functools.partial(_sb_attn_kernel, tq=tq, tk=tk, seq=seq, heads=heads),
        grid=(batch, B_HEADS // heads),
        in_specs=[spec, spec, pl.BlockSpec((None, seq // tk, heads * B_DIM, tk), lambda b, g: (b, 0, g, 0)),
                  pl.BlockSpec(later2.shape, lambda b, g: (0, 0))],
        out_specs=spec,
        out_shape=jax.ShapeDtypeStruct((batch, seq, B_HEADS * B_DIM), F32),
        scratch_shapes=[pltpu.VMEM((heads, tk, tq), F32)] * 2,
        compiler_params=_attn_params(),
        name="sb_attn",
    )(q, k, vt, later2)


def _rope_tables(seq):
    pos = jnp.arange(seq, dtype=F32)
    inv = 1.0 / (ROPE_THETA ** (jnp.arange(0, A_ROPE, 2, dtype=F32) / A_ROPE))
    ang = pos[:, None] * inv[None, :]
    cos, sin = jnp.cos(ang), jnp.sin(ang)
    ones = jnp.ones((seq, A_NOPE), F32)
    zeros_n = jnp.zeros((seq, A_NOPE), F32)
    zeros_p = jnp.zeros((seq, LANES - A_NOPE - A_ROPE), F32)
    cos_t = jnp.concatenate([ones, cos, cos, zeros_p], axis=1)
    sin_t = jnp.concatenate([zeros_n, sin, sin, zeros_p], axis=1)
    return cos_t, sin_t


def _rot_half_cols(w):
    half = A_ROPE // 2
    return jnp.concatenate([-w[..., half:], w[..., :half]], axis=-1)


def _mla_weights(w_in, w_uq, w_ukv):
    d = w_in.shape[0]
    pad = LANES - A_NOPE - A_ROPE
    c_q = w_in[:, :A_QLORA]
    c_kv = w_in[:, A_QLORA:A_QLORA + A_KVLORA]
    k_r = w_in[:, A_QLORA + A_KVLORA:A_QLORA + A_KVLORA + A_ROPE]
    w_gate = w_in[:, A_QLORA + A_KVLORA + A_ROPE:]
    zn = jnp.zeros((d, A_NOPE), F32)
    zp = jnp.zeros((d, pad), F32)
    w_in_p = jnp.concatenate([c_q, c_kv, zn, k_r, zp, zn, _rot_half_cols(k_r), zp], axis=1)

    wq = w_uq.reshape(A_QLORA, A_HEADS, A_NOPE + A_ROPE)
    nope, rope = wq[..., :A_NOPE], wq[..., A_NOPE:]
    zqn = jnp.zeros((A_QLORA, A_HEADS, A_NOPE), F32)
    zqp = jnp.zeros((A_QLORA, A_HEADS, pad), F32)
    q_a = jnp.concatenate([nope, rope, zqp], axis=-1).reshape(A_QLORA, A_HEADS * LANES)
    q_b = jnp.concatenate([zqn, _rot_half_cols(rope), zqp], axis=-1).reshape(A_QLORA, A_HEADS * LANES)
    w_q_p = jnp.concatenate([q_a, q_b], axis=1)

    wkv = w_ukv.reshape(A_KVLORA, A_HEADS, A_NOPE + A_VDIM)
    kn, vv = wkv[..., :A_NOPE], wkv[..., A_NOPE:]
    zk = jnp.zeros((A_KVLORA, A_HEADS, LANES - A_NOPE), F32)
    w_k_p = jnp.concatenate([kn, zk], axis=-1).reshape(A_KVLORA, A_HEADS * LANES)
    w_vt = vv.reshape(A_KVLORA, A_HEADS * A_VDIM).T
    return (w_in_p.astype(BF16), w_gate.astype(BF16), w_q_p.astype(BF16), w_k_p.astype(BF16),
            w_vt.astype(BF16))


def _later_key_matrix(blk):
    s = lax.broadcasted_iota(jnp.int32, (blk, blk), 0)
    j = lax.broadcasted_iota(jnp.int32, (blk, blk), 1)
    m = jnp.concatenate([(j > s).astype(BF16), jnp.ones((BF16_SUBLANES, blk), BF16)], axis=0)
    return jnp.concatenate([m, m], axis=1)


def kernel(x, a_norm_pre, a_w_in, a_q_norm, a_w_uq, a_kv_norm, a_w_ukv, a_w_o, a_norm_post,
           b_kv_norm, b_w_kv, b_norm_pre, b_w_in, b_w_o, b_norm_post):
    batch, seq, d = x.shape
    rows = batch * seq
    tq = min(ATTN_TQ, seq)
    tk = min(MLA_TK, seq)
    x2 = x.reshape(rows, d)
    cos_t, sin_t = _rope_tables(seq)
    n_a = a_w_in.shape[0]
    n_b = b_w_in.shape[0]

    for i in range(n_a):
        w_in_p, w_gate, w_q_p, w_k_p, w_vt = _mla_weights(a_w_in[i], a_w_uq[i], a_w_ukv[i])
        q, k, vt = _mla_proj(x2, a_norm_pre[i][None], w_in_p, a_q_norm[i][None], w_q_p,
                             a_kv_norm[i][None], w_k_p, w_vt, cos_t, sin_t, seq, tk)
        o = _mla_attn(q.reshape(batch, seq, -1), k.reshape(batch, seq, -1), vt, tq)
        x2 = _out_proj(x2, o.reshape(rows, -1), a_norm_pre[i][None], w_gate,
                       a_w_o[i].astype(BF16), a_norm_post[i][None])

    width = B_HEADS * B_DIM
    tk = min(SB_TK, seq)
    later2 = _later_key_matrix(min(SB_BLOCK, tk))
    k = vt = None
    for j in range(n_b):
        w_q = b_w_in[j][:, :width].astype(BF16)
        w_gate = b_w_in[j][:, width:].astype(BF16)
        if j == 0:
            q, k, vt = _sb_proj(x2, b_norm_pre[j][None], w_q, seq, tk, b_kv_norm[None],
                                b_w_kv[:, :width].astype(BF16), b_w_kv[:, width:].T.astype(BF16))
            k = k.reshape(batch, seq, width)
        else:
            (q,) = _sb_proj(x2, b_norm_pre[j][None], w_q, seq, tk)
        o = _sb_attn(q.reshape(batch, seq, width), k, vt, later2, tq)
        x2 = _out_proj(x2, o.reshape(rows, width), b_norm_pre[j][None], w_gate,
                       b_w_o[j].astype(BF16), b_norm_post[j][None])
    return x2.reshape(batch, seq, d)
```

```python
import functools
import math

import jax
import jax.numpy as jnp
from jax import lax
from jax.experimental import pallas as pl
from jax.experimental.pallas import tpu as pltpu

D_MODEL = 1024
A_HEADS = 16
A_NOPE = 64
A_ROPE = 32
A_VDIM = 64
A_QLORA = 256
A_KVLORA = 128
B_HEADS = 16
B_DIM = 64
ROPE_THETA = 10000.0
EPS = 1e-6

LANES = 128
ROW_TILE = 512
BF16_SUBLANES = 16
A_VROWS = A_VDIM + BF16_SUBLANES
ATTN_TQ = 256
MLA_TK = 512
MLA_HEADS_PER_STEP = 4
SB_TK = 256
SB_HEADS_PER_STEP = 8
SB_BLOCK = 128
SIGN_BIT = -2147483648
F32_UNDERFLOW_LOG2 = -152.0
CHUNK = 32
VMEM_LIMIT = 56 * 1024 * 1024
LOG2E = 1.4426950408889634

F32 = jnp.float32
BF16 = jnp.bfloat16


def _rms_scale(x):
    return lax.rsqrt(jnp.mean(x * x, axis=-1, keepdims=True) + EPS)


def _dot(a, b):
    return jnp.dot(a, b, preferred_element_type=F32)


def _ordered_after(x, earlier):
    zero = lax.shift_right_logical(lax.shift_right_logical(pltpu.bitcast(earlier, jnp.int32), 16), 16)
    return pltpu.bitcast(pltpu.bitcast(x, jnp.int32) | zero, F32)


def _dot_nt(a, b):
    return lax.dot_general(a, b, (((1,), (1,)), ((), ())), preferred_element_type=F32)


def _store_key_tiles(vt_ref, v_t, ones_rows=0):
    tile = vt_ref.shape[-1]
    for t in range(vt_ref.shape[0]):
        cols = slice(t * tile, (t + 1) * tile)
        if not ones_rows:
            vt_ref[t] = v_t[:, cols].astype(BF16)
            continue
        per_head = A_VDIM + ones_rows
        for hd in range(v_t.shape[0] // A_VDIM):
            vt_ref[t, hd * per_head:hd * per_head + A_VDIM, :] = v_t[hd * A_VDIM:(hd + 1) * A_VDIM, cols].astype(BF16)
            vt_ref[t, hd * per_head + A_VDIM:(hd + 1) * per_head, :] = jnp.ones((ones_rows, tile), BF16)


def _mla_proj_kernel(x_ref, g_ref, win_ref, qg_ref, wq_ref, kvg_ref, wk_ref, wvt_ref, cos_ref, sin_ref,
                     q_ref, k_ref, vt_ref):
    x = x_ref[...]
    h = x * _rms_scale(x) * g_ref[...]
    proj = _dot(h.astype(BF16), win_ref[...])
    cq = proj[:, :A_QLORA]
    ckv = proj[:, A_QLORA:A_QLORA + A_KVLORA]
    kr_a = proj[:, A_QLORA + A_KVLORA:A_QLORA + A_KVLORA + LANES]
    kr_b = proj[:, A_QLORA + A_KVLORA + LANES:]
    cqn = cq * _rms_scale(cq) * qg_ref[...]
    q2 = _dot(cqn.astype(BF16), wq_ref[...])
    ckvn = (ckv * _rms_scale(ckv) * kvg_ref[...]).astype(BF16)
    kn = _dot(ckvn, wk_ref[...])
    _store_key_tiles(vt_ref, _dot_nt(wvt_ref[...], ckvn), BF16_SUBLANES)
    cos = cos_ref[...]
    sin = sin_ref[...]
    kr = kr_a * cos + kr_b * sin
    scale = LOG2E / math.sqrt(A_NOPE + A_ROPE)
    cos_q = cos * scale
    sin_q = sin * scale
    hw = A_HEADS * LANES
    for hd in range(A_HEADS):
        sl = slice(hd * LANES, (hd + 1) * LANES)
        sl_rot = slice(hw + hd * LANES, hw + (hd + 1) * LANES)
        q_ref[:, sl] = (q2[:, sl] * cos_q + q2[:, sl_rot] * sin_q).astype(BF16)
        k_ref[:, sl] = (kn[:, sl] + kr).astype(BF16)


def _sb_proj_kernel(*refs, with_kv):
    if with_kv:
        x_ref, gq_ref, wq_ref, gkv_ref, wk_ref, wvt_ref, q_ref, k_ref, vt_ref = refs
    else:
        x_ref, gq_ref, wq_ref, q_ref = refs
    x = x_ref[...]
    y = x * _rms_scale(x)
    q = _dot((y * gq_ref[...]).astype(BF16), wq_ref[...])
    q_ref[...] = (q * (LOG2E / math.sqrt(B_DIM))).astype(BF16)
    if with_kv:
        h_kv = (y * gkv_ref[...]).astype(BF16)
        k_ref[...] = _dot(h_kv, wk_ref[...]).astype(BF16)
        _store_key_tiles(vt_ref, _dot_nt(wvt_ref[...], h_kv))


def _out_kernel(x_ref, o_ref, gpre_ref, wg_ref, wo_ref, gpost_ref, xn_ref):
    x = x_ref[...]
    h = x * _rms_scale(x) * gpre_ref[...]
    gate = _dot(h.astype(BF16), wg_ref[...])
    og = o_ref[...] * (gate * (1.0 / (1.0 + jnp.exp(-gate))))
    out = _dot(og.astype(BF16), wo_ref[...])
    xn_ref[...] = x + out * _rms_scale(out) * gpost_ref[...]


def _row_spec(tm, width):
    return pl.BlockSpec((tm, width), lambda i: (i, 0))


def _full_spec(shape):
    return pl.BlockSpec(shape, lambda i: (0,) * len(shape))


def _row_params():
    return pltpu.CompilerParams(dimension_semantics=("parallel",), vmem_limit_bytes=VMEM_LIMIT)


def _vt_out(rows, seq, tm, tile, width):
    tiles_per_seq = seq // tm
    spec = pl.BlockSpec((None, tm // tile, width, tile), lambda i: (i // tiles_per_seq, i % tiles_per_seq, 0, 0))
    return spec, jax.ShapeDtypeStruct((rows // seq, seq // tile, width, tile), BF16)


def _mla_proj(x2, g, w_in_p, qg, w_q_p, kvg, w_k_p, w_vt, cos_t, sin_t, seq, tile):
    rows = x2.shape[0]
    tm = min(ROW_TILE, seq)
    tiles_per_seq = seq // tm
    hw = A_HEADS * LANES
    table_spec = pl.BlockSpec((tm, LANES), lambda i: (i % tiles_per_seq, 0))
    vt_spec, vt_shape = _vt_out(rows, seq, tm, tile, A_HEADS * A_VROWS)
    return pl.pallas_call(
        _mla_proj_kernel,
        grid=(rows // tm,),
        in_specs=[_row_spec(tm, D_MODEL), _full_spec(g.shape), _full_spec(w_in_p.shape),
                  _full_spec(qg.shape), _full_spec(w_q_p.shape), _full_spec(kvg.shape),
                  _full_spec(w_k_p.shape), _full_spec(w_vt.shape), table_spec, table_spec],
        out_specs=[_row_spec(tm, hw), _row_spec(tm, hw), vt_spec],
        out_shape=[jax.ShapeDtypeStruct((rows, hw), BF16), jax.ShapeDtypeStruct((rows, hw), BF16), vt_shape],
        compiler_params=_row_params(),
        name="mla_proj",
    )(x2, g, w_in_p, qg, w_q_p, kvg, w_k_p, w_vt, cos_t, sin_t)


def _sb_proj(x2, gq, w_q, seq, tile, gkv=None, w_k=None, w_vt=None):
    rows = x2.shape[0]
    tm = min(ROW_TILE, seq)
    width = B_HEADS * B_DIM
    with_kv = w_k is not None
    args = [x2, gq, w_q] + ([gkv, w_k, w_vt] if with_kv else [])
    in_specs = [_row_spec(tm, D_MODEL)] + [_full_spec(a.shape) for a in args[1:]]
    out_specs = [_row_spec(tm, width)]
    out_shape = [jax.ShapeDtypeStruct((rows, width), BF16)]
    if with_kv:
        vt_spec, vt_shape = _vt_out(rows, seq, tm, tile, width)
        out_specs += [_row_spec(tm, width), vt_spec]
        out_shape += [jax.ShapeDtypeStruct((rows, width), BF16), vt_shape]
    return pl.pallas_call(
        functools.partial(_sb_proj_kernel, with_kv=with_kv),
        grid=(rows // tm,),
        in_specs=in_specs,
        out_specs=out_specs,
        out_shape=out_shape,
        compiler_params=_row_params(),
        name="sb_proj_kv" if with_kv else "sb_proj",
    )(*args)


def _out_proj(x2, o2, gpre, w_gate, w_o, gpost):
    rows = x2.shape[0]
    tm = min(ROW_TILE, rows)
    return pl.pallas_call(
        _out_kernel,
        grid=(rows // tm,),
        in_specs=[_row_spec(tm, D_MODEL), _row_spec(tm, o2.shape[1]), _full_spec(gpre.shape),
                  _full_spec(w_gate.shape), _full_spec(w_o.shape), _full_spec(gpost.shape)],
        out_specs=_row_spec(tm, D_MODEL),
        out_shape=jax.ShapeDtypeStruct((rows, D_MODEL), F32),
        compiler_params=_row_params(),
        name="out_proj",
    )(x2, o2, gpre, w_gate, w_o, gpost)


def _mla_attn_kernel(q_ref, k_ref, vt_ref, o_ref, s_scr0, s_scr1, *, tq, tk, seq, heads):
    s_scr = (s_scr0, s_scr1)
    key = lax.broadcasted_iota(jnp.int32, (CHUNK, tq), 0)
    qry = lax.broadcasted_iota(jnp.int32, (CHUNK, tq), 1)

    def q_body(qi, _):
        r0 = pl.multiple_of(qi * tq, tq)
        qs = [q_ref[pl.ds(r0, tq), hd * LANES:(hd + 1) * LANES] for hd in range(heads)]
        diag = (qi * tq) // tk

        def scores(j, slot):
            c0 = pl.multiple_of(j * tk, tk)
            tile_max = []
            for hd in range(heads):
                s = _dot_nt(k_ref[pl.ds(c0, tk), hd * LANES:(hd + 1) * LANES], qs[hd])
                s_scr[slot][hd] = s
                tile_max.append(jnp.max(s, axis=0, keepdims=True))
            return tile_max

        def score_chunk(slot, hd, c, causal_limit):
            blk = s_scr[slot][hd, c * CHUNK:(c + 1) * CHUNK, :]
            if causal_limit is None:
                return blk
            return jnp.where(key + c * CHUNK <= causal_limit, blk, -jnp.inf)

        def tile(j, slot, carry, tile_max, causal_limit=None):
            new = []
            previous = None
            for hd, (m, acc) in enumerate(carry):
                if causal_limit is None:
                    mx = tile_max[hd]
                else:
                    mx = score_chunk(slot, hd, 0, causal_limit)
                    for c in range(1, tk // CHUNK):
                        mx = jnp.maximum(mx, score_chunk(slot, hd, c, causal_limit))
                    mx = jnp.max(mx, axis=0, keepdims=True)
                m_new = jnp.maximum(m, mx)
                if previous is not None:
                    m_new = _ordered_after(m_new, previous)
                alpha = jnp.exp2(m - m_new)
                ps = [jnp.exp2(score_chunk(slot, hd, c, causal_limit) - m_new).astype(BF16)
                      for c in range(tk // CHUNK)]
                previous = ps[-1][:1, :].astype(F32)
                vt = vt_ref[j, hd * A_VROWS:(hd + 1) * A_VROWS, :]
                acc = alpha * acc + _dot(vt, jnp.concatenate(ps, axis=0))
                new.append((m_new, acc))
            return tuple(new)

        def step(j, state, slot):
            carry, tile_max = state
            next_max = scores(j + 1, 1 - slot)
            return tile(j, slot, carry, tile_max), next_max

        def body(j, state):
            return lax.cond(j % 2 == 0, functools.partial(step, slot=0), functools.partial(step, slot=1),
                            j, state)

        def finish(carry, slot):
            carry = tile(diag, slot, carry, None, qry + (r0 - diag * tk))
            outs = [acc[:A_VDIM] / acc[A_VDIM:A_VDIM + 1] for _, acc in carry]
            o_ref[pl.ds(r0, tq), :] = jnp.concatenate(outs, axis=0).T

        first_max = scores(0, 0)
        init = tuple((jnp.full((1, tq), -jnp.inf, F32), jnp.zeros((A_VROWS, tq), F32)) for _ in range(heads))
        carry, _ = lax.fori_loop(0, diag, body, (init, first_max))

        @pl.when(diag % 2 == 0)
        def _():
            finish(carry, 0)

        @pl.when(diag % 2 == 1)
        def _():
            finish(carry, 1)

        return 0

    lax.fori_loop(0, seq // tq, q_body, 0)


def _sb_attn_kernel(q_ref, k_ref, vt_ref, later_ref, o_ref, z_scr0, z_scr1, *, tq, tk, seq, heads):
    blk = later_ref.shape[1] // 2
    z_scr = (z_scr0, z_scr1)
    lane = lax.broadcasted_iota(jnp.int32, (tq, LANES), 1)
    key = lax.broadcasted_iota(jnp.int32, (CHUNK, tq), 0)
    qry = lax.broadcasted_iota(jnp.int32, (CHUNK, tq), 1)

    def q_body(qi, _):
        r0 = pl.multiple_of(qi * tq, tq)
        qs = []
        for hd in range(heads):
            pair = q_ref[pl.ds(r0, tq), (hd // 2) * LANES:(hd // 2 + 1) * LANES]
            mine = (lane < B_DIM) if hd % 2 == 0 else (lane >= B_DIM)
            qs.append(jnp.where(mine, pair, jnp.zeros_like(pair)))
        diag = (qi * tq) // tk

        def logits(j, slot, hd):
            c0 = pl.multiple_of(j * tk, tk)
            k = k_ref[pl.ds(c0, tk), (hd // 2) * LANES:(hd // 2 + 1) * LANES]
            z_scr[slot][hd] = _dot_nt(k, qs[hd])

        def cumulative_sums(slot, hd, b, strict_limit):
            his, los = [], []
            for c in range(blk // CHUNK):
                rows = slice(b * blk + c * CHUNK, b * blk + (c + 1) * CHUNK)
                z = z_scr[slot][hd, rows, :]
                neg_abs = pltpu.bitcast(pltpu.bitcast(z, jnp.int32) | SIGN_BIT, F32)
                neg_lse = jnp.log(1.0 + jnp.exp2(neg_abs)) * (-LOG2E)
                log_rest = neg_lse - jnp.maximum(z, 0.0)
                z_scr[slot][hd, rows, :] = log_rest + z
                if strict_limit is not None:
                    log_rest = jnp.where(key + (b * blk + c * CHUNK) < strict_limit, log_rest, 0.0)
                hi = log_rest.astype(BF16)
                his.append(hi)
                los.append((log_rest - hi.astype(F32)).astype(BF16))
            return _dot(later_ref[...], jnp.concatenate(his + los, axis=0))

        def weigh_values(j, slot, hd, b, sums, later, acc, strict_limit):
            weights = []
            for c in range(blk // CHUNK):
                rows = slice(b * blk + c * CHUNK, b * blk + (c + 1) * CHUNK)
                a = jnp.exp2(z_scr[slot][hd, rows, :] + sums[c * CHUNK:(c + 1) * CHUNK, :] + later)
                if strict_limit is not None:
                    a = jnp.where(key + (b * blk + c * CHUNK) < strict_limit, a, 0.0)
                weights.append(a.astype(BF16))
            vt = vt_ref[j, hd * B_DIM:(hd + 1) * B_DIM, b * blk:(b + 1) * blk]
            return later + sums[blk:blk + 1], acc + _dot(vt, jnp.concatenate(weights, axis=0))

        def tile(j, slot, carry, strict_limit=None, next_tile=None):
            blocks = list(reversed(range(tk // blk)))
            new = []
            pending = None
            for hd in range(heads + 1):
                sums = [cumulative_sums(slot, hd, b, strict_limit) for b in blocks] if hd < heads else None
                if next_tile is not None and hd < heads:
                    logits(next_tile, 1 - slot, hd)
                if pending is not None:
                    later, acc = carry[hd - 1]
                    for b, block_sums in zip(blocks, pending):
                        later, acc = weigh_values(j, slot, hd - 1, b, block_sums, later, acc, strict_limit)
                    new.append((later, acc))
                pending = sums
            return tuple(new)

        def step(n, carry, slot):
            j = diag - n
            return tile(j, slot, carry, next_tile=jnp.maximum(j - 1, 0))

        def any_weight_left(carry):
            worst = carry[0][0]
            for later, _ in carry[1:]:
                worst = jnp.maximum(worst, later)
            return (jnp.max(worst) > F32_UNDERFLOW_LOG2).astype(jnp.int32)

        def more(state):
            n, live, _ = state
            return jnp.logical_and(n <= diag, live > 0)

        def body(state):
            n, _, carry = state
            carry = lax.cond(n % 2 == 0, functools.partial(step, slot=0), functools.partial(step, slot=1),
                             n, carry)
            return n + 1, any_weight_left(carry), carry

        for hd in range(heads):
            logits(diag, 0, hd)
        init = tuple((jnp.zeros((1, tq), F32), jnp.zeros((B_DIM, tq), F32)) for _ in range(heads))
        carry = tile(diag, 0, init, qry + (r0 - diag * tk), next_tile=jnp.maximum(diag - 1, 0))
        _, _, carry = lax.while_loop(more, body, (jnp.int32(1), any_weight_left(carry), carry))
        o_ref[pl.ds(r0, tq), :] = jnp.concatenate([acc for _, acc in carry], axis=0).T
        return 0

    lax.fori_loop(0, seq // tq, q_body, 0)


def _attn_params():
    return pltpu.CompilerParams(dimension_semantics=("parallel", "parallel"), vmem_limit_bytes=VMEM_LIMIT)


def _mla_attn(q, k, vt, tq):
    batch, seq, _ = q.shape
    tk = vt.shape[-1]
    heads = MLA_HEADS_PER_STEP
    qk_spec = pl.BlockSpec((None, seq, heads * LANES), lambda b, g: (b, 0, g))
    return pl.pallas_call(
        functools.partial(_mla_attn_kernel, tq=tq, tk=tk, seq=seq, heads=heads),
        grid=(batch, A_HEADS // heads),
        in_specs=[qk_spec, qk_spec,
                  pl.BlockSpec((None, seq // tk, heads * A_VROWS, tk), lambda b, g: (b, 0, g, 0))],
        out_specs=pl.BlockSpec((None, seq, heads * A_VDIM), lambda b, g: (b, 0, g)),
        out_shape=jax.ShapeDtypeStruct((batch, seq, A_HEADS * A_VDIM), F32),
        scratch_shapes=[pltpu.VMEM((heads, tk, tq), F32)] * 2,
        compiler_params=_attn_params(),
        name="mla_attn",
    )(q, k, vt)


def _sb_attn(q, k, vt, later2, tq):
    batch, seq, _ = q.shape
    tk = vt.shape[-1]
    heads = SB_HEADS_PER_STEP
    spec = pl.BlockSpec((None, seq, heads * B_DIM), lambda b, g: (b, 0, g))
    return pl.pallas_call(
        functools.partial(_sb_attn_kernel, tq=tq, tk=tk, seq=seq, heads=heads),
        grid=(batch, B_HEADS // heads),
        in_specs=[spec, spec, pl.BlockSpec((None, seq // tk, heads * B_DIM, tk), lambda b, g: (b, 0, g, 0)),
                  pl.BlockSpec(later2.shape, lambda b, g: (0, 0))],
        out_specs=spec,
        out_shape=jax.ShapeDtypeStruct((batch, seq, B_HEADS * B_DIM), F32),
        scratch_shapes=[pltpu.VMEM((heads, tk, tq), F32)] * 2,
        compiler_params=_attn_params(),
        name="sb_attn",
    )(q, k, vt, later2)


def _rope_tables(seq):
    pos = jnp.arange(seq, dtype=F32)
    inv = 1.0 / (ROPE_THETA ** (jnp.arange(0, A_ROPE, 2, dtype=F32) / A_ROPE))
    ang = pos[:, None] * inv[None, :]
    cos, sin = jnp.cos(ang), jnp.sin(ang)
    ones = jnp.ones((seq, A_NOPE), F32)
    zeros_n = jnp.zeros((seq, A_NOPE), F32)
    zeros_p = jnp.zeros((seq, LANES - A_NOPE - A_ROPE), F32)
    cos_t = jnp.concatenate([ones, cos, cos, zeros_p], axis=1)
    sin_t = jnp.concatenate([zeros_n, sin, sin, zeros_p], axis=1)
    return cos_t, sin_t


def _rot_half_cols(w):
    half = A_ROPE // 2
    return jnp.concatenate([-w[..., half:], w[..., :half]], axis=-1)


def _mla_weights(w_in, w_uq, w_ukv):
    d = w_in.shape[0]
    pad = LANES - A_NOPE - A_ROPE
    c_q = w_in[:, :A_QLORA]
    c_kv = w_in[:, A_QLORA:A_QLORA + A_KVLORA]
    k_r = w_in[:, A_QLORA + A_KVLORA:A_QLORA + A_KVLORA + A_ROPE]
    w_gate = w_in[:, A_QLORA + A_KVLORA + A_ROPE:]
    zn = jnp.zeros((d, A_NOPE), F32)
    zp = jnp.zeros((d, pad), F32)
    w_in_p = jnp.concatenate([c_q, c_kv, zn, k_r, zp, zn, _rot_half_cols(k_r), zp], axis=1)

    wq = w_uq.reshape(A_QLORA, A_HEADS, A_NOPE + A_ROPE)
    nope, rope = wq[..., :A_NOPE], wq[..., A_NOPE:]
    zqn = jnp.zeros((A_QLORA, A_HEADS, A_NOPE), F32)
    zqp = jnp.zeros((A_QLORA, A_HEADS, pad), F32)
    q_a = jnp.concatenate([nope, rope, zqp], axis=-1).reshape(A_QLORA, A_HEADS * LANES)
    q_b = jnp.concatenate([zqn, _rot_half_cols(rope), zqp], axis=-1).reshape(A_QLORA, A_HEADS * LANES)
    w_q_p = jnp.concatenate([q_a, q_b], axis=1)

    wkv = w_ukv.reshape(A_KVLORA, A_HEADS, A_NOPE + A_VDIM)
    kn, vv = wkv[..., :A_NOPE], wkv[..., A_NOPE:]
    zk = jnp.zeros((A_KVLORA, A_HEADS, LANES - A_NOPE), F32)
    w_k_p = jnp.concatenate([kn, zk], axis=-1).reshape(A_KVLORA, A_HEADS * LANES)
    w_vt = vv.reshape(A_KVLORA, A_HEADS * A_VDIM).T
    return (w_in_p.astype(BF16), w_gate.astype(BF16), w_q_p.astype(BF16), w_k_p.astype(BF16),
            w_vt.astype(BF16))


def _later_key_matrix(blk):
    s = lax.broadcasted_iota(jnp.int32, (blk, blk), 0)
    j = lax.broadcasted_iota(jnp.int32, (blk, blk), 1)
    m = jnp.concatenate([(j > s).astype(BF16), jnp.ones((BF16_SUBLANES, blk), BF16)], axis=0)
    return jnp.concatenate([m, m], axis=1)


def kernel(x, a_norm_pre, a_w_in, a_q_norm, a_w_uq, a_kv_norm, a_w_ukv, a_w_o, a_norm_post,
           b_kv_norm, b_w_kv, b_norm_pre, b_w_in, b_w_o, b_norm_post):
    batch, seq, d = x.shape
    rows = batch * seq
    tq = min(ATTN_TQ, seq)
    tk = min(MLA_TK, seq)
    x2 = x.reshape(rows, d)
    cos_t, sin_t = _rope_tables(seq)
    n_a = a_w_in.shape[0]
    n_b = b_w_in.shape[0]

    for i in range(n_a):
        w_in_p, w_gate, w_q_p, w_k_p, w_vt = _mla_weights(a_w_in[i], a_w_uq[i], a_w_ukv[i])
        q, k, vt = _mla_proj(x2, a_norm_pre[i][None], w_in_p, a_q_norm[i][None], w_q_p,
                             a_kv_norm[i][None], w_k_p, w_vt, cos_t, sin_t, seq, tk)
        o = _mla_attn(q.reshape(batch, seq, -1), k.reshape(batch, seq, -1), vt, tq)
        x2 = _out_proj(x2, o.reshape(rows, -1), a_norm_pre[i][None], w_gate,
                       a_w_o[i].astype(BF16), a_norm_post[i][None])

    width = B_HEADS * B_DIM
    tk = min(SB_TK, seq)
    later2 = _later_key_matrix(min(SB_BLOCK, tk))
    k = vt = None
    for j in range(n_b):
        w_q = b_w_in[j][:, :width].astype(BF16)
        w_gate = b_w_in[j][:, width:].astype(BF16)
        if j == 0:
            q, k, vt = _sb_proj(x2, b_norm_pre[j][None], w_q, seq, tk, b_kv_norm[None],
                                b_w_kv[:, :width].astype(BF16), b_w_kv[:, width:].T.astype(BF16))
            k = k.reshape(batch, seq, width)
        else:
            (q,) = _sb_proj(x2, b_norm_pre[j][None], w_q, seq, tk)
        o = _sb_attn(q.reshape(batch, seq, width), k, vt, later2, tq)
        x2 = _out_proj(x2, o.reshape(rows, width), b_norm_pre[j][None], w_gate,
                       b_w_o[j].astype(BF16), b_norm_post[j][None])
    return x2.reshape(batch, seq, d)
```

```python
import functools
import math

import jax
import jax.numpy as jnp
from jax import lax
from jax.experimental import pallas as pl
from jax.experimental.pallas import tpu as pltpu

D_MODEL = 1024
A_HEADS = 16
A_NOPE = 64
A_ROPE = 32
A_VDIM = 64
A_QLORA = 256
A_KVLORA = 128
B_HEADS = 16
B_DIM = 64
ROPE_THETA = 10000.0
EPS = 1e-6

LANES = 128
ROW_TILE = 512
BF16_SUBLANES = 16
A_VROWS = A_VDIM + BF16_SUBLANES
ATTN_TQ = 256
MLA_TK = 512
MLA_HEADS_PER_STEP = 4
PREFETCH_HEADS = 2
SB_TK = 256
SB_HEADS_PER_STEP = 8
SB_BLOCK = 128
SIGN_BIT = -2147483648
F32_UNDERFLOW_LOG2 = -152.0
CHUNK = 32
VMEM_LIMIT = 56 * 1024 * 1024
LOG2E = 1.4426950408889634

F32 = jnp.float32
BF16 = jnp.bfloat16


def _rms_scale(x):
    return lax.rsqrt(jnp.mean(x * x, axis=-1, keepdims=True) + EPS)


def _dot(a, b):
    return jnp.dot(a, b, preferred_element_type=F32)


def _dot_nt(a, b):
    return lax.dot_general(a, b, (((1,), (1,)), ((), ())), preferred_element_type=F32)


def _store_key_tiles(vt_ref, v_t, ones_rows=0):
    tile = vt_ref.shape[-1]
    for t in range(vt_ref.shape[0]):
        cols = slice(t * tile, (t + 1) * tile)
        if not ones_rows:
            vt_ref[t] = v_t[:, cols].astype(BF16)
            continue
        per_head = A_VDIM + ones_rows
        for hd in range(v_t.shape[0] // A_VDIM):
            vt_ref[t, hd * per_head:hd * per_head + A_VDIM, :] = v_t[hd * A_VDIM:(hd + 1) * A_VDIM, cols].astype(BF16)
            vt_ref[t, hd * per_head + A_VDIM:(hd + 1) * per_head, :] = jnp.ones((ones_rows, tile), BF16)


def _mla_proj_kernel(x_ref, g_ref, win_ref, qg_ref, wq_ref, kvg_ref, wk_ref, wvt_ref, cos_ref, sin_ref,
                     q_ref, k_ref, vt_ref):
    x = x_ref[...]
    h = x * _rms_scale(x) * g_ref[...]
    proj = _dot(h.astype(BF16), win_ref[...])
    cq = proj[:, :A_QLORA]
    ckv = proj[:, A_QLORA:A_QLORA + A_KVLORA]
    kr_a = proj[:, A_QLORA + A_KVLORA:A_QLORA + A_KVLORA + LANES]
    kr_b = proj[:, A_QLORA + A_KVLORA + LANES:]
    cqn = cq * _rms_scale(cq) * qg_ref[...]
    q2 = _dot(cqn.astype(BF16), wq_ref[...])
    ckvn = (ckv * _rms_scale(ckv) * kvg_ref[...]).astype(BF16)
    kn = _dot(ckvn, wk_ref[...])
    _store_key_tiles(vt_ref, _dot_nt(wvt_ref[...], ckvn), BF16_SUBLANES)
    cos = cos_ref[...]
    sin = sin_ref[...]
    kr = kr_a * cos + kr_b * sin
    scale = LOG2E / math.sqrt(A_NOPE + A_ROPE)
    cos_q = cos * scale
    sin_q = sin * scale
    hw = A_HEADS * LANES
    for hd in range(A_HEADS):
        sl = slice(hd * LANES, (hd + 1) * LANES)
        sl_rot = slice(hw + hd * LANES, hw + (hd + 1) * LANES)
        q_ref[:, sl] = (q2[:, sl] * cos_q + q2[:, sl_rot] * sin_q).astype(BF16)
        k_ref[:, sl] = (kn[:, sl] + kr).astype(BF16)


def _sb_proj_kernel(*refs, with_kv):
    if with_kv:
        x_ref, gq_ref, wq_ref, gkv_ref, wk_ref, wvt_ref, q_ref, k_ref, vt_ref = refs
    else:
        x_ref, gq_ref, wq_ref, q_ref = refs
    x = x_ref[...]
    y = x * _rms_scale(x)
    q = _dot((y * gq_ref[...]).astype(BF16), wq_ref[...])
    q_ref[...] = (q * (LOG2E / math.sqrt(B_DIM))).astype(BF16)
    if with_kv:
        h_kv = (y * gkv_ref[...]).astype(BF16)
        k_ref[...] = _dot(h_kv, wk_ref[...]).astype(BF16)
        _store_key_tiles(vt_ref, _dot_nt(wvt_ref[...], h_kv))


def _out_kernel(x_ref, o_ref, gpre_ref, wg_ref, wo_ref, gpost_ref, xn_ref):
    x = x_ref[...]
    h = x * _rms_scale(x) * gpre_ref[...]
    gate = _dot(h.astype(BF16), wg_ref[...])
    og = o_ref[...] * (gate * (1.0 / (1.0 + jnp.exp(-gate))))
    out = _dot(og.astype(BF16), wo_ref[...])
    xn_ref[...] = x + out * _rms_scale(out) * gpost_ref[...]


def _row_spec(tm, width):
    return pl.BlockSpec((tm, width), lambda i: (i, 0))


def _full_spec(shape):
    return pl.BlockSpec(shape, lambda i: (0,) * len(shape))


def _row_params():
    return pltpu.CompilerParams(dimension_semantics=("parallel",), vmem_limit_bytes=VMEM_LIMIT)


def _vt_out(rows, seq, tm, tile, width):
    tiles_per_seq = seq // tm
    spec = pl.BlockSpec((None, tm // tile, width, tile), lambda i: (i // tiles_per_seq, i % tiles_per_seq, 0, 0))
    return spec, jax.ShapeDtypeStruct((rows // seq, seq // tile, width, tile), BF16)


def _mla_proj(x2, g, w_in_p, qg, w_q_p, kvg, w_k_p, w_vt, cos_t, sin_t, seq, tile):
    rows = x2.shape[0]
    tm = min(ROW_TILE, seq)
    tiles_per_seq = seq // tm
    hw = A_HEADS * LANES
    table_spec = pl.BlockSpec((tm, LANES), lambda i: (i % tiles_per_seq, 0))
    vt_spec, vt_shape = _vt_out(rows, seq, tm, tile, A_HEADS * A_VROWS)
    return pl.pallas_call(
        _mla_proj_kernel,
        grid=(rows // tm,),
        in_specs=[_row_spec(tm, D_MODEL), _full_spec(g.shape), _full_spec(w_in_p.shape),
                  _full_spec(qg.shape), _full_spec(w_q_p.shape), _full_spec(kvg.shape),
                  _full_spec(w_k_p.shape), _full_spec(w_vt.shape), table_spec, table_spec],
        out_specs=[_row_spec(tm, hw), _row_spec(tm, hw), vt_spec],
        out_shape=[jax.ShapeDtypeStruct((rows, hw), BF16), jax.ShapeDtypeStruct((rows, hw), BF16), vt_shape],
        compiler_params=_row_params(),
        name="mla_proj",
    )(x2, g, w_in_p, qg, w_q_p, kvg, w_k_p, w_vt, cos_t, sin_t)


def _sb_proj(x2, gq, w_q, seq, tile, gkv=None, w_k=None, w_vt=None):
    rows = x2.shape[0]
    tm = min(ROW_TILE, seq)
    width = B_HEADS * B_DIM
    with_kv = w_k is not None
    args = [x2, gq, w_q] + ([gkv, w_k, w_vt] if with_kv else [])
    in_specs = [_row_spec(tm, D_MODEL)] + [_full_spec(a.shape) for a in args[1:]]
    out_specs = [_row_spec(tm, width)]
    out_shape = [jax.ShapeDtypeStruct((rows, width), BF16)]
    if with_kv:
        vt_spec, vt_shape = _vt_out(rows, seq, tm, tile, width)
        out_specs += [_row_spec(tm, width), vt_spec]
        out_shape += [jax.ShapeDtypeStruct((rows, width), BF16), vt_shape]
    return pl.pallas_call(
        functools.partial(_sb_proj_kernel, with_kv=with_kv),
        grid=(rows // tm,),
        in_specs=in_specs,
        out_specs=out_specs,
        out_shape=out_shape,
        compiler_params=_row_params(),
        name="sb_proj_kv" if with_kv else "sb_proj",
    )(*args)


def _out_proj(x2, o2, gpre, w_gate, w_o, gpost):
    rows = x2.shape[0]
    tm = min(ROW_TILE, rows)
    return pl.pallas_call(
        _out_kernel,
        grid=(rows // tm,),
        in_specs=[_row_spec(tm, D_MODEL), _row_spec(tm, o2.shape[1]), _full_spec(gpre.shape),
                  _full_spec(w_gate.shape), _full_spec(w_o.shape), _full_spec(gpost.shape)],
        out_specs=_row_spec(tm, D_MODEL),
        out_shape=jax.ShapeDtypeStruct((rows, D_MODEL), F32),
        compiler_params=_row_params(),
        name="out_proj",
    )(x2, o2, gpre, w_gate, w_o, gpost)


def _mla_attn_kernel(q_ref, k_ref, vt_ref, o_ref, s_scr0, s_scr1, *, tq, tk, seq, heads):
    s_scr = (s_scr0, s_scr1)
    n_query_tiles = seq // tq
    key = lax.broadcasted_iota(jnp.int32, (CHUNK, tq), 0)
    qry = lax.broadcasted_iota(jnp.int32, (CHUNK, tq), 1)

    def load_queries(qi):
        r0 = pl.multiple_of(qi * tq, tq)
        return [q_ref[pl.ds(r0, tq), hd * LANES:(hd + 1) * LANES] for hd in range(heads)]

    def score_head(qs, j, slot, hd):
        c0 = pl.multiple_of(j * tk, tk)
        s = _dot_nt(k_ref[pl.ds(c0, tk), hd * LANES:(hd + 1) * LANES], qs[hd])
        s_scr[slot][hd] = s
        return jnp.max(s, axis=0, keepdims=True)

    def score_chunk(slot, hd, c, causal_limit):
        blk = s_scr[slot][hd, c * CHUNK:(c + 1) * CHUNK, :]
        if causal_limit is None:
            return blk
        return jnp.where(key + c * CHUNK <= causal_limit, blk, -jnp.inf)

    def tile(j, slot, carry, tile_max, prefetch, causal_limit=None):
        new, next_max = [], []
        pending = None
        for hd, (m, acc) in enumerate(carry):
            for ahead in range(PREFETCH_HEADS * hd, min(PREFETCH_HEADS * (hd + 1), heads)):
                next_max.append(prefetch(ahead))
            if causal_limit is None:
                mx = tile_max[hd]
            else:
                mx = score_chunk(slot, hd, 0, causal_limit)
                for c in range(1, tk // CHUNK):
                    mx = jnp.maximum(mx, score_chunk(slot, hd, c, causal_limit))
                mx = jnp.max(mx, axis=0, keepdims=True)
            m_new = jnp.maximum(m, mx)
            alpha = jnp.exp2(m - m_new)
            ps = [jnp.exp2(score_chunk(slot, hd, c, causal_limit) - m_new).astype(BF16)
                  for c in range(tk // CHUNK)]
            pv = _dot(vt_ref[j, hd * A_VROWS:(hd + 1) * A_VROWS, :], jnp.concatenate(ps, axis=0))
            if pending is not None:
                new.append((pending[0], pending[1] * pending[2] + pending[3]))
            pending = (m_new, alpha, acc, pv)
        new.append((pending[0], pending[1] * pending[2] + pending[3]))
        return tuple(new), next_max

    def q_body(qi, outer):
        first_buffer, first_max = outer
        r0 = pl.multiple_of(qi * tq, tq)
        qs = load_queries(qi)
        diag = (qi * tq) // tk

        def step(j, state, slot):
            carry, tile_max = state
            return tile(j, slot, carry, tile_max, lambda hd: score_head(qs, j + 1, 1 - slot, hd))

        def body(j, state):
            return lax.cond((j + first_buffer) % 2 == 0, functools.partial(step, slot=0),
                            functools.partial(step, slot=1), j, state)

        def finish(carry, slot):
            qs_next = load_queries(jnp.minimum(qi + 1, n_query_tiles - 1))
            carry, next_first_max = tile(diag, slot, carry, None,
                                         lambda hd: score_head(qs_next, 0, 1 - slot, hd),
                                         qry + (r0 - diag * tk))
            outs = [acc[:A_VDIM] / acc[A_VDIM:A_VDIM + 1] for _, acc in carry]
            o_ref[pl.ds(r0, tq), :] = jnp.concatenate(outs, axis=0).T
            return next_first_max

        init = tuple((jnp.full((1, tq), -jnp.inf, F32), jnp.zeros((A_VROWS, tq), F32)) for _ in range(heads))
        carry, _ = lax.fori_loop(0, diag, body, (init, first_max))
        diag_buffer = (diag + first_buffer) % 2
        next_first_max = lax.cond(diag_buffer == 0, functools.partial(finish, slot=0),
                                  functools.partial(finish, slot=1), carry)
        return 1 - diag_buffer, next_first_max

    qs0 = load_queries(0)
    lax.fori_loop(0, n_query_tiles, q_body, (jnp.int32(0), [score_head(qs0, 0, 0, hd) for hd in range(heads)]))


def _sb_attn_kernel(q_ref, k_ref, vt_ref, later_ref, o_ref, z_scr0, z_scr1, *, tq, tk, seq, heads):
    blk = later_ref.shape[1] // 2
    z_scr = (z_scr0, z_scr1)
    lane = lax.broadcasted_iota(jnp.int32, (tq, LANES), 1)
    key = lax.broadcasted_iota(jnp.int32, (CHUNK, tq), 0)
    qry = lax.broadcasted_iota(jnp.int32, (CHUNK, tq), 1)

    def q_body(qi, _):
        r0 = pl.multiple_of(qi * tq, tq)
        qs = []
        for hd in range(heads):
            pair = q_ref[pl.ds(r0, tq), (hd // 2) * LANES:(hd // 2 + 1) * LANES]
            mine = (lane < B_DIM) if hd % 2 == 0 else (lane >= B_DIM)
            qs.append(jnp.where(mine, pair, jnp.zeros_like(pair)))
        diag = (qi * tq) // tk

        def logits(j, slot, hd):
            c0 = pl.multiple_of(j * tk, tk)
            k = k_ref[pl.ds(c0, tk), (hd // 2) * LANES:(hd // 2 + 1) * LANES]
            z_scr[slot][hd] = _dot_nt(k, qs[hd])

        def cumulative_sums(slot, hd, b, strict_limit):
            his, los = [], []
            for c in range(blk // CHUNK):
                rows = slice(b * blk + c * CHUNK, b * blk + (c + 1) * CHUNK)
                z = z_scr[slot][hd, rows, :]
                neg_abs = pltpu.bitcast(pltpu.bitcast(z, jnp.int32) | SIGN_BIT, F32)
                neg_lse = jnp.log(1.0 + jnp.exp2(neg_abs)) * (-LOG2E)
                log_rest = neg_lse - jnp.maximum(z, 0.0)
                z_scr[slot][hd, rows, :] = log_rest + z
                if strict_limit is not None:
                    log_rest = jnp.where(key + (b * blk + c * CHUNK) < strict_limit, log_rest, 0.0)
                hi = log_rest.astype(BF16)
                his.append(hi)
                los.append((log_rest - hi.astype(F32)).astype(BF16))
            return _dot(later_ref[...], jnp.concatenate(his + los, axis=0))

        def weigh_values(j, slot, hd, b, sums, later, acc, strict_limit):
            weights = []
            for c in range(blk // CHUNK):
                rows = slice(b * blk + c * CHUNK, b * blk + (c + 1) * CHUNK)
                a = jnp.exp2(z_scr[slot][hd, rows, :] + sums[c * CHUNK:(c + 1) * CHUNK, :] + later)
                if strict_limit is not None:
                    a = jnp.where(key + (b * blk + c * CHUNK) < strict_limit, a, 0.0)
                weights.append(a.astype(BF16))
            vt = vt_ref[j, hd * B_DIM:(hd + 1) * B_DIM, b * blk:(b + 1) * blk]
            return later + sums[blk:blk + 1], acc + _dot(vt, jnp.concatenate(weights, axis=0))

        def tile(j, slot, carry, strict_limit=None, next_tile=None):
            blocks = list(reversed(range(tk // blk)))
            new = []
            pending = None
            for hd in range(heads + 1):
                sums = [cumulative_sums(slot, hd, b, strict_limit) for b in blocks] if hd < heads else None
                if next_tile is not None and hd < heads:
                    logits(next_tile, 1 - slot, hd)
                if pending is not None:
                    later, acc = carry[hd - 1]
                    for b, block_sums in zip(blocks, pending):
                        later, acc = weigh_values(j, slot, hd - 1, b, block_sums, later, acc, strict_limit)
                    new.append((later, acc))
                pending = sums
            return tuple(new)

        def step(n, carry, slot):
            j = diag - n
            return tile(j, slot, carry, next_tile=jnp.maximum(j - 1, 0))

        def any_weight_left(carry):
            worst = carry[0][0]
            for later, _ in carry[1:]:
                worst = jnp.maximum(worst, later)
            return (jnp.max(worst) > F32_UNDERFLOW_LOG2).astype(jnp.int32)

        def more(state):
            n, live, _ = state
            return jnp.logical_and(n <= diag, live > 0)

        def body(state):
            n, _, carry = state
            carry = lax.cond(n % 2 == 0, functools.partial(step, slot=0), functools.partial(step, slot=1),
                             n, carry)
            return n + 1, any_weight_left(carry), carry

        for hd in range(heads):
            logits(diag, 0, hd)
        init = tuple((jnp.zeros((1, tq), F32), jnp.zeros((B_DIM, tq), F32)) for _ in range(heads))
        carry = tile(diag, 0, init, qry + (r0 - diag * tk), next_tile=jnp.maximum(diag - 1, 0))
        _, _, carry = lax.while_loop(more, body, (jnp.int32(1), any_weight_left(carry), carry))
        o_ref[pl.ds(r0, tq), :] = jnp.concatenate([acc for _, acc in carry], axis=0).T
        return 0

    lax.fori_loop(0, seq // tq, q_body, 0)


def _attn_params():
    return pltpu.CompilerParams(dimension_semantics=("parallel", "parallel"), vmem_limit_bytes=VMEM_LIMIT)


def _mla_attn(q, k, vt, tq):
    batch, seq, _ = q.shape
    tk = vt.shape[-1]
    heads = MLA_HEADS_PER_STEP
    qk_spec = pl.BlockSpec((None, seq, heads * LANES), lambda b, g: (b, 0, g))
    return pl.pallas_call(
        functools.partial(_mla_attn_kernel, tq=tq, tk=tk, seq=seq, heads=heads),
        grid=(batch, A_HEADS // heads),
        in_specs=[qk_spec, qk_spec,
                  pl.BlockSpec((None, seq // tk, heads * A_VROWS, tk), lambda b, g: (b, 0, g, 0))],
        out_specs=pl.BlockSpec((None, seq, heads * A_VDIM), lambda b, g: (b, 0, g)),
        out_shape=jax.ShapeDtypeStruct((batch, seq, A_HEADS * A_VDIM), F32),
        scratch_shapes=[pltpu.VMEM((heads, tk, tq), F32)] * 2,
        compiler_params=_attn_params(),
        name="mla_attn",
    )(q, k, vt)


def _sb_attn(q, k, vt, later2, tq):
    batch, seq, _ = q.shape
    tk = vt.shape[-1]
    heads = SB_HEADS_PER_STEP
    spec = pl.BlockSpec((None, seq, heads * B_DIM), lambda b, g: (b, 0, g))
    return pl.pallas_call(
        functools.partial(_sb_attn_kernel, tq=tq, tk=tk, seq=seq, heads=heads),
        grid=(batch, B_HEADS // heads),
        in_specs=[spec, spec, pl.BlockSpec((None, seq // tk, heads * B_DIM, tk), lambda b, g: (b, 0, g, 0)),
                  pl.BlockSpec(later2.shape, lambda b, g: (0, 0))],
        out_specs=spec,
        out_shape=jax.ShapeDtypeStruct((batch, seq, B_HEADS * B_DIM), F32),
        scratch_shapes=[pltpu.VMEM((heads, tk, tq), F32)] * 2,
        compiler_params=_attn_params(),
        name="sb_attn",
    )(q, k, vt, later2)


def _rope_tables(seq):
    pos = jnp.arange(seq, dtype=F32)
    inv = 1.0 / (ROPE_THETA ** (jnp.arange(0, A_ROPE, 2, dtype=F32) / A_ROPE))
    ang = pos[:, None] * inv[None, :]
    cos, sin = jnp.cos(ang), jnp.sin(ang)
    ones = jnp.ones((seq, A_NOPE), F32)
    zeros_n = jnp.zeros((seq, A_NOPE), F32)
    zeros_p = jnp.zeros((seq, LANES - A_NOPE - A_ROPE), F32)
    cos_t = jnp.concatenate([ones, cos, cos, zeros_p], axis=1)
    sin_t = jnp.concatenate([zeros_n, sin, sin, zeros_p], axis=1)
    return cos_t, sin_t


def _rot_half_cols(w):
    half = A_ROPE // 2
    return jnp.concatenate([-w[..., half:], w[..., :half]], axis=-1)


def _mla_weights(w_in, w_uq, w_ukv):
    d = w_in.shape[0]
    pad = LANES - A_NOPE - A_ROPE
    c_q = w_in[:, :A_QLORA]
    c_kv = w_in[:, A_QLORA:A_QLORA + A_KVLORA]
    k_r = w_in[:, A_QLORA + A_KVLORA:A_QLORA + A_KVLORA + A_ROPE]
    w_gate = w_in[:, A_QLORA + A_KVLORA + A_ROPE:]
    zn = jnp.zeros((d, A_NOPE), F32)
    zp = jnp.zeros((d, pad), F32)
    w_in_p = jnp.concatenate([c_q, c_kv, zn, k_r, zp, zn, _rot_half_cols(k_r), zp], axis=1)

    wq = w_uq.reshape(A_QLORA, A_HEADS, A_NOPE + A_ROPE)
    nope, rope = wq[..., :A_NOPE], wq[..., A_NOPE:]
    zqn = jnp.zeros((A_QLORA, A_HEADS, A_NOPE), F32)
    zqp = jnp.zeros((A_QLORA, A_HEADS, pad), F32)
    q_a = jnp.concatenate([nope, rope, zqp], axis=-1).reshape(A_QLORA, A_HEADS * LANES)
    q_b = jnp.concatenate([zqn, _rot_half_cols(rope), zqp], axis=-1).reshape(A_QLORA, A_HEADS * LANES)
    w_q_p = jnp.concatenate([q_a, q_b], axis=1)

    wkv = w_ukv.reshape(A_KVLORA, A_HEADS, A_NOPE + A_VDIM)
    kn, vv = wkv[..., :A_NOPE], wkv[..., A_NOPE:]
    zk = jnp.zeros((A_KVLORA, A_HEADS, LANES - A_NOPE), F32)
    w_k_p = jnp.concatenate([kn, zk], axis=-1).reshape(A_KVLORA, A_HEADS * LANES)
    w_vt = vv.reshape(A_KVLORA, A_HEADS * A_VDIM).T
    return (w_in_p.astype(BF16), w_gate.astype(BF16), w_q_p.astype(BF16), w_k_p.astype(BF16),
            w_vt.astype(BF16))


def _later_key_matrix(blk):
    s = lax.broadcasted_iota(jnp.int32, (blk, blk), 0)
    j = lax.broadcasted_iota(jnp.int32, (blk, blk), 1)
    m = jnp.concatenate([(j > s).astype(BF16), jnp.ones((BF16_SUBLANES, blk), BF16)], axis=0)
    return jnp.concatenate([m, m], axis=1)


def kernel(x, a_norm_pre, a_w_in, a_q_norm, a_w_uq, a_kv_norm, a_w_ukv, a_w_o, a_norm_post,
           b_kv_norm, b_w_kv, b_norm_pre, b_w_in, b_w_o, b_norm_post):
    batch, seq, d = x.shape
    rows = batch * seq
    tq = min(ATTN_TQ, seq)
    tk = min(MLA_TK, seq)
    x2 = x.reshape(rows, d)
    cos_t, sin_t = _rope_tables(seq)
    n_a = a_w_in.shape[0]
    n_b = b_w_in.shape[0]

    for i in range(n_a):
        w_in_p, w_gate, w_q_p, w_k_p, w_vt = _mla_weights(a_w_in[i], a_w_uq[i], a_w_ukv[i])
        q, k, vt = _mla_proj(x2, a_norm_pre[i][None], w_in_p, a_q_norm[i][None], w_q_p,
                             a_kv_norm[i][None], w_k_p, w_vt, cos_t, sin_t, seq, tk)
        o = _mla_attn(q.reshape(batch, seq, -1), k.reshape(batch, seq, -1), vt, tq)
        x2 = _out_proj(x2, o.reshape(rows, -1), a_norm_pre[i][None], w_gate,
                       a_w_o[i].astype(BF16), a_norm_post[i][None])

    width = B_HEADS * B_DIM
    tk = min(SB_TK, seq)
    later2 = _later_key_matrix(min(SB_BLOCK, tk))
    k = vt = None
    for j in range(n_b):
        w_q = b_w_in[j][:, :width].astype(BF16)
        w_gate = b_w_in[j][:, width:].astype(BF16)
        if j == 0:
            q, k, vt = _sb_proj(x2, b_norm_pre[j][None], w_q, seq, tk, b_kv_norm[None],
                                b_w_kv[:, :width].astype(BF16), b_w_kv[:, width:].T.astype(BF16))
            k = k.reshape(batch, seq, width)
        else:
            (q,) = _sb_proj(x2, b_norm_pre[j][None], w_q, seq, tk)
        o = _sb_attn(q.reshape(batch, seq, width), k, vt, later2, tq)
        x2 = _out_proj(x2, o.reshape(rows, width), b_norm_pre[j][None], w_gate,
                       b_w_o[j].astype(BF16), b_norm_post[j][None])
    return x2.reshape(batch, seq, d)
```

```python
import functools
import math

import jax
import jax.numpy as jnp
from jax import lax
from jax.experimental import pallas as pl
from jax.experimental.pallas import tpu as pltpu

D_MODEL = 1024
A_HEADS = 16
A_NOPE = 64
A_ROPE = 32
A_VDIM = 64
A_QLORA = 256
A_KVLORA = 128
B_HEADS = 16
B_DIM = 64
ROPE_THETA = 10000.0
EPS = 1e-6

LANES = 128
ROW_TILE = 512
OUT_ROW_TILE = 1024
ROW_PARTS = 2
BF16_SUBLANES = 16
A_VROWS = A_VDIM + BF16_SUBLANES
ATTN_TQ = 256
MLA_TK = 512
MLA_HEADS_PER_STEP = 4
PREFETCH_HEADS = 2
SB_TK = 256
SB_HEADS_PER_STEP = 8
SB_BLOCK = 128
SIGN_BIT = -2147483648
F32_UNDERFLOW_LOG2 = -152.0
CHUNK = 32
VMEM_LIMIT = 56 * 1024 * 1024
LOG2E = 1.4426950408889634

F32 = jnp.float32
BF16 = jnp.bfloat16


def _rms_scale(x):
    return lax.rsqrt(jnp.mean(x * x, axis=-1, keepdims=True) + EPS)


def _dot(a, b):
    return jnp.dot(a, b, preferred_element_type=F32)


def _dot_nt(a, b):
    return lax.dot_general(a, b, (((1,), (1,)), ((), ())), preferred_element_type=F32)


def _store_key_tiles(vt_ref, v_t, ones_rows=0):
    tile = vt_ref.shape[-1]
    for t in range(vt_ref.shape[0]):
        cols = slice(t * tile, (t + 1) * tile)
        if not ones_rows:
            vt_ref[t] = v_t[:, cols].astype(BF16)
            continue
        per_head = A_VDIM + ones_rows
        for hd in range(v_t.shape[0] // A_VDIM):
            vt_ref[t, hd * per_head:hd * per_head + A_VDIM, :] = v_t[hd * A_VDIM:(hd + 1) * A_VDIM, cols].astype(BF16)
            vt_ref[t, hd * per_head + A_VDIM:(hd + 1) * per_head, :] = jnp.ones((ones_rows, tile), BF16)


def _mla_proj_kernel(x_ref, g_ref, win_ref, qg_ref, wq_ref, kvg_ref, wk_ref, wvt_ref, cos_ref, sin_ref,
                     q_ref, k_ref, vt_ref):
    x = x_ref[...]
    h = x * _rms_scale(x) * g_ref[...]
    proj = _dot(h.astype(BF16), win_ref[...])
    cq = proj[:, :A_QLORA]
    ckv = proj[:, A_QLORA:A_QLORA + A_KVLORA]
    kr_a = proj[:, A_QLORA + A_KVLORA:A_QLORA + A_KVLORA + LANES]
    kr_b = proj[:, A_QLORA + A_KVLORA + LANES:]
    cqn = cq * _rms_scale(cq) * qg_ref[...]
    q2 = _dot(cqn.astype(BF16), wq_ref[...])
    ckvn = (ckv * _rms_scale(ckv) * kvg_ref[...]).astype(BF16)
    kn = _dot(ckvn, wk_ref[...])
    _store_key_tiles(vt_ref, _dot_nt(wvt_ref[...], ckvn), BF16_SUBLANES)
    cos = cos_ref[...]
    sin = sin_ref[...]
    kr = kr_a * cos + kr_b * sin
    scale = LOG2E / math.sqrt(A_NOPE + A_ROPE)
    cos_q = cos * scale
    sin_q = sin * scale
    hw = A_HEADS * LANES
    for hd in range(A_HEADS):
        sl = slice(hd * LANES, (hd + 1) * LANES)
        sl_rot = slice(hw + hd * LANES, hw + (hd + 1) * LANES)
        q_ref[:, sl] = (q2[:, sl] * cos_q + q2[:, sl_rot] * sin_q).astype(BF16)
        k_ref[:, sl] = (kn[:, sl] + kr).astype(BF16)


def _sb_proj_kernel(*refs, with_kv):
    if with_kv:
        x_ref, gq_ref, wq_ref, gkv_ref, wk_ref, wvt_ref, q_ref, k_ref, vt_ref = refs
    else:
        x_ref, gq_ref, wq_ref, q_ref = refs
    x = x_ref[...]
    y = x * _rms_scale(x)
    q = _dot((y * gq_ref[...]).astype(BF16), wq_ref[...])
    q_ref[...] = (q * (LOG2E / math.sqrt(B_DIM))).astype(BF16)
    if with_kv:
        h_kv = (y * gkv_ref[...]).astype(BF16)
        k_ref[...] = _dot(h_kv, wk_ref[...]).astype(BF16)
        _store_key_tiles(vt_ref, _dot_nt(wvt_ref[...], h_kv))


def _row_parts(tm):
    part = tm // ROW_PARTS
    return [slice(i * part, (i + 1) * part) for i in range(ROW_PARTS)]


def _out_kernel(x_ref, o_ref, gpre_ref, wg_ref, wo_ref, gpost_ref, xn_ref):
    parts = _row_parts(x_ref.shape[0])
    gates = []
    for rows in parts:
        x = x_ref[rows, :]
        h = x * _rms_scale(x) * gpre_ref[...]
        gates.append(_dot(h.astype(BF16), wg_ref[...]))
    outs = []
    for rows, gate in zip(parts, gates):
        og = o_ref[rows, :] * (gate * (1.0 / (1.0 + jnp.exp(-gate))))
        outs.append(_dot(og.astype(BF16), wo_ref[...]))
    for rows, out in zip(parts, outs):
        xn_ref[rows, :] = x_ref[rows, :] + out * _rms_scale(out) * gpost_ref[...]


def _row_spec(tm, width):
    return pl.BlockSpec((tm, width), lambda i: (i, 0))


def _full_spec(shape):
    return pl.BlockSpec(shape, lambda i: (0,) * len(shape))


def _row_params():
    return pltpu.CompilerParams(dimension_semantics=("parallel",), vmem_limit_bytes=VMEM_LIMIT)


def _vt_out(rows, seq, tm, tile, width):
    tiles_per_seq = seq // tm
    spec = pl.BlockSpec((None, tm // tile, width, tile), lambda i: (i // tiles_per_seq, i % tiles_per_seq, 0, 0))
    return spec, jax.ShapeDtypeStruct((rows // seq, seq // tile, width, tile), BF16)


def _mla_proj(x2, g, w_in_p, qg, w_q_p, kvg, w_k_p, w_vt, cos_t, sin_t, seq, tile):
    rows = x2.shape[0]
    tm = min(ROW_TILE, seq)
    tiles_per_seq = seq // tm
    hw = A_HEADS * LANES
    table_spec = pl.BlockSpec((tm, LANES), lambda i: (i % tiles_per_seq, 0))
    vt_spec, vt_shape = _vt_out(rows, seq, tm, tile, A_HEADS * A_VROWS)
    return pl.pallas_call(
        _mla_proj_kernel,
        grid=(rows // tm,),
        in_specs=[_row_spec(tm, D_MODEL), _full_spec(g.shape), _full_spec(w_in_p.shape),
                  _full_spec(qg.shape), _full_spec(w_q_p.shape), _full_spec(kvg.shape),
                  _full_spec(w_k_p.shape), _full_spec(w_vt.shape), table_spec, table_spec],
        out_specs=[_row_spec(tm, hw), _row_spec(tm, hw), vt_spec],
        out_shape=[jax.ShapeDtypeStruct((rows, hw), BF16), jax.ShapeDtypeStruct((rows, hw), BF16), vt_shape],
        compiler_params=_row_params(),
        name="mla_proj",
    )(x2, g, w_in_p, qg, w_q_p, kvg, w_k_p, w_vt, cos_t, sin_t)


def _sb_proj(x2, gq, w_q, seq, tile, gkv=None, w_k=None, w_vt=None):
    rows = x2.shape[0]
    tm = min(ROW_TILE, seq)
    width = B_HEADS * B_DIM
    with_kv = w_k is not None
    args = [x2, gq, w_q] + ([gkv, w_k, w_vt] if with_kv else [])
    in_specs = [_row_spec(tm, D_MODEL)] + [_full_spec(a.shape) for a in args[1:]]
    out_specs = [_row_spec(tm, width)]
    out_shape = [jax.ShapeDtypeStruct((rows, width), BF16)]
    if with_kv:
        vt_spec, vt_shape = _vt_out(rows, seq, tm, tile, width)
        out_specs += [_row_spec(tm, width), vt_spec]
        out_shape += [jax.ShapeDtypeStruct((rows, width), BF16), vt_shape]
    return pl.pallas_call(
        functools.partial(_sb_proj_kernel, with_kv=with_kv),
        grid=(rows // tm,),
        in_specs=in_specs,
        out_specs=out_specs,
        out_shape=out_shape,
        compiler_params=_row_params(),
        name="sb_proj_kv" if with_kv else "sb_proj",
    )(*args)


def _out_proj(x2, o2, gpre, w_gate, w_o, gpost):
    rows = x2.shape[0]
    tm = min(OUT_ROW_TILE, rows)
    return pl.pallas_call(
        _out_kernel,
        grid=(rows // tm,),
        in_specs=[_row_spec(tm, D_MODEL), _row_spec(tm, o2.shape[1]), _full_spec(gpre.shape),
                  _full_spec(w_gate.shape), _full_spec(w_o.shape), _full_spec(gpost.shape)],
        out_specs=_row_spec(tm, D_MODEL),
        out_shape=jax.ShapeDtypeStruct((rows, D_MODEL), F32),
        compiler_params=_row_params(),
        name="out_proj",
    )(x2, o2, gpre, w_gate, w_o, gpost)


def _mla_attn_kernel(q_ref, k_ref, vt_ref, o_ref, s_scr, max_scr, m_scr, acc_scr, *, tq, tk, seq, heads):
    n_query_tiles = seq // tq
    key = lax.broadcasted_iota(jnp.int32, (CHUNK, tq), 0)
    qry = lax.broadcasted_iota(jnp.int32, (CHUNK, tq), 1)

    def load_queries(qi):
        r0 = pl.multiple_of(qi * tq, tq)
        return [q_ref[pl.ds(r0, tq), hd * LANES:(hd + 1) * LANES] for hd in range(heads)]

    def score_head(qs, j, slot, hd):
        c0 = pl.multiple_of(j * tk, tk)
        s = _dot_nt(k_ref[pl.ds(c0, tk), hd * LANES:(hd + 1) * LANES], qs[hd])
        s_scr[slot, hd] = s
        max_scr[slot, hd] = jnp.max(s, axis=0, keepdims=True)

    def score_chunk(slot, hd, c, causal_limit):
        blk = s_scr[slot, hd, c * CHUNK:(c + 1) * CHUNK, :]
        if causal_limit is None:
            return blk
        return jnp.where(key + c * CHUNK <= causal_limit, blk, -jnp.inf)

    def tile(j, slot, prefetch, causal_limit=None):
        pending = None
        for hd in range(heads):
            for ahead in range(PREFETCH_HEADS * hd, min(PREFETCH_HEADS * (hd + 1), heads)):
                prefetch(ahead)
            if causal_limit is None:
                mx = max_scr[slot, hd]
            else:
                mx = score_chunk(slot, hd, 0, causal_limit)
                for c in range(1, tk // CHUNK):
                    mx = jnp.maximum(mx, score_chunk(slot, hd, c, causal_limit))
                mx = jnp.max(mx, axis=0, keepdims=True)
            m = m_scr[hd]
            m_new = jnp.maximum(m, mx)
            m_scr[hd] = m_new
            alpha = jnp.exp2(m - m_new)
            ps = [jnp.exp2(score_chunk(slot, hd, c, causal_limit) - m_new).astype(BF16)
                  for c in range(tk // CHUNK)]
            pv = _dot(vt_ref[j, hd * A_VROWS:(hd + 1) * A_VROWS, :], jnp.concatenate(ps, axis=0))
            if pending is not None:
                acc_scr[pending[0]] = pending[1] * acc_scr[pending[0]] + pending[2]
            pending = (hd, alpha, pv)
        acc_scr[pending[0]] = pending[1] * acc_scr[pending[0]] + pending[2]

    def q_body(qi, first_buffer):
        r0 = pl.multiple_of(qi * tq, tq)
        qs = load_queries(qi)
        diag = (qi * tq) // tk
        for hd in range(heads):
            m_scr[hd] = jnp.full((1, tq), -jnp.inf, F32)
            acc_scr[hd] = jnp.zeros((A_VROWS, tq), F32)

        def body(j, _):
            for slot in range(2):
                @pl.when((j + first_buffer) % 2 == slot)
                def _():
                    tile(j, slot, lambda hd: score_head(qs, j + 1, 1 - slot, hd))
            return 0

        lax.fori_loop(0, diag, body, 0)
        diag_buffer = (diag + first_buffer) % 2
        for slot in range(2):
            @pl.when(diag_buffer == slot)
            def _():
                qs_next = load_queries(jnp.minimum(qi + 1, n_query_tiles - 1))
                tile(diag, slot, lambda hd: score_head(qs_next, 0, 1 - slot, hd), qry + (r0 - diag * tk))
                outs = [acc_scr[hd, :A_VDIM, :] / acc_scr[hd, A_VDIM:A_VDIM + 1, :] for hd in range(heads)]
                o_ref[pl.ds(r0, tq), :] = jnp.concatenate(outs, axis=0).T
        return 1 - diag_buffer

    qs0 = load_queries(0)
    for hd in range(heads):
        score_head(qs0, 0, 0, hd)
    lax.fori_loop(0, n_query_tiles, q_body, jnp.int32(0))


def _sb_attn_kernel(q_ref, k_ref, vt_ref, later_ref, o_ref, z_scr, later_scr, acc_scr, *, tq, tk, seq, heads):
    blk = later_ref.shape[1] // 2
    lane = lax.broadcasted_iota(jnp.int32, (tq, LANES), 1)
    key = lax.broadcasted_iota(jnp.int32, (CHUNK, tq), 0)
    qry = lax.broadcasted_iota(jnp.int32, (CHUNK, tq), 1)

    def q_body(qi, _):
        r0 = pl.multiple_of(qi * tq, tq)
        qs = []
        for hd in range(heads):
            pair = q_ref[pl.ds(r0, tq), (hd // 2) * LANES:(hd // 2 + 1) * LANES]
            mine = (lane < B_DIM) if hd % 2 == 0 else (lane >= B_DIM)
            qs.append(jnp.where(mine, pair, jnp.zeros_like(pair)))
        diag = (qi * tq) // tk

        def logits(j, slot, hd):
            c0 = pl.multiple_of(j * tk, tk)
            k = k_ref[pl.ds(c0, tk), (hd // 2) * LANES:(hd // 2 + 1) * LANES]
            z_scr[slot, hd] = _dot_nt(k, qs[hd])

        def cumulative_sums(slot, hd, b, strict_limit):
            his, los = [], []
            for c in range(blk // CHUNK):
                rows = slice(b * blk + c * CHUNK, b * blk + (c + 1) * CHUNK)
                z = z_scr[slot, hd, rows, :]
                neg_abs = pltpu.bitcast(pltpu.bitcast(z, jnp.int32) | SIGN_BIT, F32)
                neg_lse = jnp.log(1.0 + jnp.exp2(neg_abs)) * (-LOG2E)
                log_rest = neg_lse - jnp.maximum(z, 0.0)
                z_scr[slot, hd, rows, :] = log_rest + z
                if strict_limit is not None:
                    log_rest = jnp.where(key + (b * blk + c * CHUNK) < strict_limit, log_rest, 0.0)
                hi = log_rest.astype(BF16)
                his.append(hi)
                los.append((log_rest - hi.astype(F32)).astype(BF16))
            return _dot(later_ref[...], jnp.concatenate(his + los, axis=0))

        def weigh_values(j, slot, hd, b, sums, later, strict_limit):
            weights = []
            for c in range(blk // CHUNK):
                rows = slice(b * blk + c * CHUNK, b * blk + (c + 1) * CHUNK)
                a = jnp.exp2(z_scr[slot, hd, rows, :] + sums[c * CHUNK:(c + 1) * CHUNK, :] + later)
                if strict_limit is not None:
                    a = jnp.where(key + (b * blk + c * CHUNK) < strict_limit, a, 0.0)
                weights.append(a.astype(BF16))
            vt = vt_ref[j, hd * B_DIM:(hd + 1) * B_DIM, b * blk:(b + 1) * blk]
            return later + sums[blk:blk + 1], _dot(vt, jnp.concatenate(weights, axis=0))

        def tile(j, slot, next_tile, strict_limit=None):
            blocks = list(reversed(range(tk // blk)))
            pending = None
            for hd in range(heads + 1):
                sums = [cumulative_sums(slot, hd, b, strict_limit) for b in blocks] if hd < heads else None
                if hd < heads:
                    logits(next_tile, 1 - slot, hd)
                if pending is not None:
                    later = later_scr[hd - 1]
                    for b, block_sums in zip(blocks, pending):
                        later, weighted = weigh_values(j, slot, hd - 1, b, block_sums, later, strict_limit)
                        acc_scr[hd - 1] += weighted
                    later_scr[hd - 1] = later
                pending = sums

        def any_weight_left():
            worst = later_scr[0]
            for hd in range(1, heads):
                worst = jnp.maximum(worst, later_scr[hd])
            return (jnp.max(worst) > F32_UNDERFLOW_LOG2).astype(jnp.int32)

        def more(state):
            n, live = state
            return jnp.logical_and(n <= diag, live > 0)

        def body(state):
            n, _ = state
            j = diag - n
            for slot in range(2):
                @pl.when(n % 2 == slot)
                def _():
                    tile(j, slot, jnp.maximum(j - 1, 0))
            return n + 1, any_weight_left()

        for hd in range(heads):
            logits(diag, 0, hd)
            later_scr[hd] = jnp.zeros((1, tq), F32)
            acc_scr[hd] = jnp.zeros((B_DIM, tq), F32)
        tile(diag, 0, jnp.maximum(diag - 1, 0), qry + (r0 - diag * tk))
        lax.while_loop(more, body, (jnp.int32(1), any_weight_left()))
        o_ref[pl.ds(r0, tq), :] = jnp.concatenate([acc_scr[hd] for hd in range(heads)], axis=0).T
        return 0

    lax.fori_loop(0, seq // tq, q_body, 0)


def _attn_params():
    return pltpu.CompilerParams(dimension_semantics=("parallel", "parallel"), vmem_limit_bytes=VMEM_LIMIT)


def _mla_attn(q, k, vt, tq):
    batch, seq, _ = q.shape
    tk = vt.shape[-1]
    heads = MLA_HEADS_PER_STEP
    qk_spec = pl.BlockSpec((None, seq, heads * LANES), lambda b, g: (b, 0, g))
    return pl.pallas_call(
        functools.partial(_mla_attn_kernel, tq=tq, tk=tk, seq=seq, heads=heads),
        grid=(batch, A_HEADS // heads),
        in_specs=[qk_spec, qk_spec,
                  pl.BlockSpec((None, seq // tk, heads * A_VROWS, tk), lambda b, g: (b, 0, g, 0))],
        out_specs=pl.BlockSpec((None, seq, heads * A_VDIM), lambda b, g: (b, 0, g)),
        out_shape=jax.ShapeDtypeStruct((batch, seq, A_HEADS * A_VDIM), F32),
        scratch_shapes=[pltpu.VMEM((2, heads, tk, tq), F32), pltpu.VMEM((2, heads, 1, tq), F32),
                        pltpu.VMEM((heads, 1, tq), F32), pltpu.VMEM((heads, A_VROWS, tq), F32)],
        compiler_params=_attn_params(),
        name="mla_attn",
    )(q, k, vt)


def _sb_attn(q, k, vt, later2, tq):
    batch, seq, _ = q.shape
    tk = vt.shape[-1]
    heads = SB_HEADS_PER_STEP
    spec = pl.BlockSpec((None, seq, heads * B_DIM), lambda b, g: (b, 0, g))
    return pl.pallas_call(
        functools.partial(_sb_attn_kernel, tq=tq, tk=tk, seq=seq, heads=heads),
        grid=(batch, B_HEADS // heads),
        in_specs=[spec, spec, pl.BlockSpec((None, seq // tk, heads * B_DIM, tk), lambda b, g: (b, 0, g, 0)),
                  pl.BlockSpec(later2.shape, lambda b, g: (0, 0))],
        out_specs=spec,
        out_shape=jax.ShapeDtypeStruct((batch, seq, B_HEADS * B_DIM), F32),
        scratch_shapes=[pltpu.VMEM((2, heads, tk, tq), F32), pltpu.VMEM((heads, 1, tq), F32),
                        pltpu.VMEM((heads, B_DIM, tq), F32)],
        compiler_params=_attn_params(),
        name="sb_attn",
    )(q, k, vt, later2)


def _rope_tables(seq):
    pos = jnp.arange(seq, dtype=F32)
    inv = 1.0 / (ROPE_THETA ** (jnp.arange(0, A_ROPE, 2, dtype=F32) / A_ROPE))
    ang = pos[:, None] * inv[None, :]
    cos, sin = jnp.cos(ang), jnp.sin(ang)
    ones = jnp.ones((seq, A_NOPE), F32)
    zeros_n = jnp.zeros((seq, A_NOPE), F32)
    zeros_p = jnp.zeros((seq, LANES - A_NOPE - A_ROPE), F32)
    cos_t = jnp.concatenate([ones, cos, cos, zeros_p], axis=1)
    sin_t = jnp.concatenate([zeros_n, sin, sin, zeros_p], axis=1)
    return cos_t, sin_t


def _rot_half_cols(w):
    half = A_ROPE // 2
    return jnp.concatenate([-w[..., half:], w[..., :half]], axis=-1)


def _mla_weights(w_in, w_uq, w_ukv):
    d = w_in.shape[0]
    pad = LANES - A_NOPE - A_ROPE
    c_q = w_in[:, :A_QLORA]
    c_kv = w_in[:, A_QLORA:A_QLORA + A_KVLORA]
    k_r = w_in[:, A_QLORA + A_KVLORA:A_QLORA + A_KVLORA + A_ROPE]
    w_gate = w_in[:, A_QLORA + A_KVLORA + A_ROPE:]
    zn = jnp.zeros((d, A_NOPE), F32)
    zp = jnp.zeros((d, pad), F32)
    w_in_p = jnp.concatenate([c_q, c_kv, zn, k_r, zp, zn, _rot_half_cols(k_r), zp], axis=1)

    wq = w_uq.reshape(A_QLORA, A_HEADS, A_NOPE + A_ROPE)
    nope, rope = wq[..., :A_NOPE], wq[..., A_NOPE:]
    zqn = jnp.zeros((A_QLORA, A_HEADS, A_NOPE), F32)
    zqp = jnp.zeros((A_QLORA, A_HEADS, pad), F32)
    q_a = jnp.concatenate([nope, rope, zqp], axis=-1).reshape(A_QLORA, A_HEADS * LANES)
    q_b = jnp.concatenate([zqn, _rot_half_cols(rope), zqp], axis=-1).reshape(A_QLORA, A_HEADS * LANES)
    w_q_p = jnp.concatenate([q_a, q_b], axis=1)

    wkv = w_ukv.reshape(A_KVLORA, A_HEADS, A_NOPE + A_VDIM)
    kn, vv = wkv[..., :A_NOPE], wkv[..., A_NOPE:]
    zk = jnp.zeros((A_KVLORA, A_HEADS, LANES - A_NOPE), F32)
    w_k_p = jnp.concatenate([kn, zk], axis=-1).reshape(A_KVLORA, A_HEADS * LANES)
    w_vt = vv.reshape(A_KVLORA, A_HEADS * A_VDIM).T
    return (w_in_p.astype(BF16), w_gate.astype(BF16), w_q_p.astype(BF16), w_k_p.astype(BF16),
            w_vt.astype(BF16))


def _later_key_matrix(blk):
    s = lax.broadcasted_iota(jnp.int32, (blk, blk), 0)
    j = lax.broadcasted_iota(jnp.int32, (blk, blk), 1)
    m = jnp.concatenate([(j > s).astype(BF16), jnp.ones((BF16_SUBLANES, blk), BF16)], axis=0)
    return jnp.concatenate([m, m], axis=1)


def kernel(x, a_norm_pre, a_w_in, a_q_norm, a_w_uq, a_kv_norm, a_w_ukv, a_w_o, a_norm_post,
           b_kv_norm, b_w_kv, b_norm_pre, b_w_in, b_w_o, b_norm_post):
    batch, seq, d = x.shape
    rows = batch * seq
    tq = min(ATTN_TQ, seq)
    tk = min(MLA_TK, seq)
    x2 = x.reshape(rows, d)
    cos_t, sin_t = _rope_tables(seq)
    n_a = a_w_in.shape[0]
    n_b = b_w_in.shape[0]

    for i in range(n_a):
        w_in_p, w_gate, w_q_p, w_k_p, w_vt = _mla_weights(a_w_in[i], a_w_uq[i], a_w_ukv[i])
        q, k, vt = _mla_proj(x2, a_norm_pre[i][None], w_in_p, a_q_norm[i][None], w_q_p,
                             a_kv_norm[i][None], w_k_p, w_vt, cos_t, sin_t, seq, tk)
        o = _mla_attn(q.reshape(batch, seq, -1), k.reshape(batch, seq, -1), vt, tq)
        x2 = _out_proj(x2, o.reshape(rows, -1), a_norm_pre[i][None], w_gate,
                       a_w_o[i].astype(BF16), a_norm_post[i][None])

    width = B_HEADS * B_DIM
    tk = min(SB_TK, seq)
    later2 = _later_key_matrix(min(SB_BLOCK, tk))
    k = vt = None
    for j in range(n_b):
        w_q = b_w_in[j][:, :width].astype(BF16)
        w_gate = b_w_in[j][:, width:].astype(BF16)
        if j == 0:
            q, k, vt = _sb_proj(x2, b_norm_pre[j][None], w_q, seq, tk, b_kv_norm[None],
                                b_w_kv[:, :width].astype(BF16), b_w_kv[:, width:].T.astype(BF16))
            k = k.reshape(batch, seq, width)
        else:
            (q,) = _sb_proj(x2, b_norm_pre[j][None], w_q, seq, tk)
        o = _sb_attn(q.reshape(batch, seq, width), k, vt, later2, tq)
        x2 = _out_proj(x2, o.reshape(rows, width), b_norm_pre[j][None], w_gate,
                       b_w_o[j].astype(BF16), b_norm_post[j][None])
    return x2.reshape(batch, seq, d)
```

```python
import functools
import math

import jax
import jax.numpy as jnp
from jax import lax
from jax.experimental import pallas as pl
from jax.experimental.pallas import tpu as pltpu

D_MODEL = 1024
A_HEADS = 16
A_NOPE = 64
A_ROPE = 32
A_VDIM = 64
A_QLORA = 256
A_KVLORA = 128
B_HEADS = 16
B_DIM = 64
ROPE_THETA = 10000.0
EPS = 1e-6

LANES = 128
ROW_TILE = 512
OUT_ROW_TILE = 1024
ROW_PARTS = 2
BF16_SUBLANES = 16
A_VROWS = A_VDIM + BF16_SUBLANES
ATTN_TQ = 256
MLA_TK = 512
MLA_HEADS_PER_STEP = 4
PREFETCH_HEADS = 2
SB_TK = 256
SB_HEADS_PER_STEP = 8
SB_BLOCK = 128
SIGN_BIT = -2147483648
F32_UNDERFLOW_LOG2 = -152.0
CHUNK = 32
VMEM_LIMIT = 56 * 1024 * 1024
LOG2E = 1.4426950408889634

F32 = jnp.float32
BF16 = jnp.bfloat16


def _rms_scale(x):
    return lax.rsqrt(jnp.mean(x * x, axis=-1, keepdims=True) + EPS)


def _dot(a, b):
    return jnp.dot(a, b, preferred_element_type=F32)


def _dot_nt(a, b):
    return lax.dot_general(a, b, (((1,), (1,)), ((), ())), preferred_element_type=F32)


def _store_key_tiles(vt_ref, v_t, ones_rows=0):
    tile = vt_ref.shape[-1]
    for t in range(vt_ref.shape[0]):
        cols = slice(t * tile, (t + 1) * tile)
        if not ones_rows:
            vt_ref[t] = v_t[:, cols].astype(BF16)
            continue
        per_head = A_VDIM + ones_rows
        for hd in range(v_t.shape[0] // A_VDIM):
            vt_ref[t, hd * per_head:hd * per_head + A_VDIM, :] = v_t[hd * A_VDIM:(hd + 1) * A_VDIM, cols].astype(BF16)
            vt_ref[t, hd * per_head + A_VDIM:(hd + 1) * per_head, :] = jnp.ones((ones_rows, tile), BF16)


def _mla_proj_kernel(x_ref, g_ref, win_ref, qg_ref, wq_ref, kvg_ref, wk_ref, wvt_ref, cos_ref, sin_ref,
                     q_ref, k_ref, vt_ref):
    x = x_ref[...]
    h = x * _rms_scale(x) * g_ref[...]
    proj = _dot(h.astype(BF16), win_ref[...])
    cq = proj[:, :A_QLORA]
    ckv = proj[:, A_QLORA:A_QLORA + A_KVLORA]
    kr_a = proj[:, A_QLORA + A_KVLORA:A_QLORA + A_KVLORA + LANES]
    kr_b = proj[:, A_QLORA + A_KVLORA + LANES:]
    cqn = cq * _rms_scale(cq) * qg_ref[...]
    q2 = _dot(cqn.astype(BF16), wq_ref[...])
    ckvn = (ckv * _rms_scale(ckv) * kvg_ref[...]).astype(BF16)
    kn = _dot(ckvn, wk_ref[...])
    _store_key_tiles(vt_ref, _dot_nt(wvt_ref[...], ckvn), BF16_SUBLANES)
    cos = cos_ref[...]
    sin = sin_ref[...]
    kr = kr_a * cos + kr_b * sin
    scale = LOG2E / math.sqrt(A_NOPE + A_ROPE)
    cos_q = cos * scale
    sin_q = sin * scale
    hw = A_HEADS * LANES
    for hd in range(A_HEADS):
        sl = slice(hd * LANES, (hd + 1) * LANES)
        sl_rot = slice(hw + hd * LANES, hw + (hd + 1) * LANES)
        q_ref[:, sl] = (q2[:, sl] * cos_q + q2[:, sl_rot] * sin_q).astype(BF16)
        k_ref[:, sl] = (kn[:, sl] + kr).astype(BF16)


def _sb_proj_kernel(*refs, with_kv):
    if with_kv:
        x_ref, gq_ref, wq_ref, gkv_ref, wk_ref, wvt_ref, q_ref, k_ref, vt_ref = refs
    else:
        x_ref, gq_ref, wq_ref, q_ref = refs
    x = x_ref[...]
    y = x * _rms_scale(x)
    q = _dot((y * gq_ref[...]).astype(BF16), wq_ref[...])
    q_ref[...] = (q * (LOG2E / math.sqrt(B_DIM))).astype(BF16)
    if with_kv:
        h_kv = (y * gkv_ref[...]).astype(BF16)
        k_ref[...] = _dot(h_kv, wk_ref[...]).astype(BF16)
        _store_key_tiles(vt_ref, _dot_nt(wvt_ref[...], h_kv))


def _row_parts(tm):
    part = tm // ROW_PARTS
    return [slice(i * part, (i + 1) * part) for i in range(ROW_PARTS)]


def _out_kernel(x_ref, o_ref, gpre_ref, wg_ref, wo_ref, gpost_ref, xn_ref):
    parts = _row_parts(x_ref.shape[0])
    gates = []
    for rows in parts:
        x = x_ref[rows, :]
        h = x * _rms_scale(x) * gpre_ref[...]
        gates.append(_dot(h.astype(BF16), wg_ref[...]))
    outs = []
    for rows, gate in zip(parts, gates):
        og = o_ref[rows, :] * (gate * (1.0 / (1.0 + jnp.exp(-gate))))
        outs.append(_dot(og.astype(BF16), wo_ref[...]))
    for rows, out in zip(parts, outs):
        xn_ref[rows, :] = x_ref[rows, :] + out * _rms_scale(out) * gpost_ref[...]


def _row_spec(tm, width):
    return pl.BlockSpec((tm, width), lambda i: (i, 0))


def _full_spec(shape):
    return pl.BlockSpec(shape, lambda i: (0,) * len(shape))


def _row_params():
    return pltpu.CompilerParams(dimension_semantics=("parallel",), vmem_limit_bytes=VMEM_LIMIT)


def _vt_out(rows, seq, tm, tile, width):
    tiles_per_seq = seq // tm
    spec = pl.BlockSpec((None, tm // tile, width, tile), lambda i: (i // tiles_per_seq, i % tiles_per_seq, 0, 0))
    return spec, jax.ShapeDtypeStruct((rows // seq, seq // tile, width, tile), BF16)


def _mla_proj(x2, g, w_in_p, qg, w_q_p, kvg, w_k_p, w_vt, cos_t, sin_t, seq, tile):
    rows = x2.shape[0]
    tm = min(ROW_TILE, seq)
    tiles_per_seq = seq // tm
    hw = A_HEADS * LANES
    table_spec = pl.BlockSpec((tm, LANES), lambda i: (i % tiles_per_seq, 0))
    vt_spec, vt_shape = _vt_out(rows, seq, tm, tile, A_HEADS * A_VROWS)
    return pl.pallas_call(
        _mla_proj_kernel,
        grid=(rows // tm,),
        in_specs=[_row_spec(tm, D_MODEL), _full_spec(g.shape), _full_spec(w_in_p.shape),
                  _full_spec(qg.shape), _full_spec(w_q_p.shape), _full_spec(kvg.shape),
                  _full_spec(w_k_p.shape), _full_spec(w_vt.shape), table_spec, table_spec],
        out_specs=[_row_spec(tm, hw), _row_spec(tm, hw), vt_spec],
        out_shape=[jax.ShapeDtypeStruct((rows, hw), BF16), jax.ShapeDtypeStruct((rows, hw), BF16), vt_shape],
        compiler_params=_row_params(),
        name="mla_proj",
    )(x2, g, w_in_p, qg, w_q_p, kvg, w_k_p, w_vt, cos_t, sin_t)


def _sb_proj(x2, gq, w_q, seq, tile, gkv=None, w_k=None, w_vt=None):
    rows = x2.shape[0]
    tm = min(ROW_TILE, seq)
    width = B_HEADS * B_DIM
    with_kv = w_k is not None
    args = [x2, gq, w_q] + ([gkv, w_k, w_vt] if with_kv else [])
    in_specs = [_row_spec(tm, D_MODEL)] + [_full_spec(a.shape) for a in args[1:]]
    out_specs = [_row_spec(tm, width)]
    out_shape = [jax.ShapeDtypeStruct((rows, width), BF16)]
    if with_kv:
        vt_spec, vt_shape = _vt_out(rows, seq, tm, tile, width)
        out_specs += [_row_spec(tm, width), vt_spec]
        out_shape += [jax.ShapeDtypeStruct((rows, width), BF16), vt_shape]
    return pl.pallas_call(
        functools.partial(_sb_proj_kernel, with_kv=with_kv),
        grid=(rows // tm,),
        in_specs=in_specs,
        out_specs=out_specs,
        out_shape=out_shape,
        compiler_params=_row_params(),
        name="sb_proj_kv" if with_kv else "sb_proj",
    )(*args)


def _out_proj(x2, o2, gpre, w_gate, w_o, gpost):
    rows = x2.shape[0]
    tm = min(OUT_ROW_TILE, rows)
    return pl.pallas_call(
        _out_kernel,
        grid=(rows // tm,),
        in_specs=[_row_spec(tm, D_MODEL), _row_spec(tm, o2.shape[1]), _full_spec(gpre.shape),
                  _full_spec(w_gate.shape), _full_spec(w_o.shape), _full_spec(gpost.shape)],
        out_specs=_row_spec(tm, D_MODEL),
        out_shape=jax.ShapeDtypeStruct((rows, D_MODEL), F32),
        compiler_params=_row_params(),
        name="out_proj",
    )(x2, o2, gpre, w_gate, w_o, gpost)


def _mla_attn_kernel(q_ref, k_ref, vt_ref, o_ref, s_scr, max_scr, m_scr, acc_scr, *, tq, tk, seq, heads):
    n_query_tiles = seq // tq
    key = lax.broadcasted_iota(jnp.int32, (CHUNK, tq), 0)
    qry = lax.broadcasted_iota(jnp.int32, (CHUNK, tq), 1)

    def load_queries(qi):
        r0 = pl.multiple_of(qi * tq, tq)
        return [q_ref[pl.ds(r0, tq), hd * LANES:(hd + 1) * LANES] for hd in range(heads)]

    def score_head(qs, j, slot, hd):
        c0 = pl.multiple_of(j * tk, tk)
        s = _dot_nt(k_ref[pl.ds(c0, tk), hd * LANES:(hd + 1) * LANES], qs[hd])
        s_scr[slot, hd] = s
        max_scr[slot, hd] = jnp.max(s, axis=0, keepdims=True)

    def score_chunk(slot, hd, c, diag_offset):
        blk = s_scr[slot, hd, c * CHUNK:(c + 1) * CHUNK, :]
        if diag_offset is None or (c + 1) * CHUNK - 1 <= diag_offset:
            return blk
        return jnp.where(key + (c * CHUNK - diag_offset) <= qry, blk, -jnp.inf)

    def tile(j, slot, prefetch, diag_offset=None):
        n_chunks = tk // CHUNK if diag_offset is None else (diag_offset + tq) // CHUNK
        pending = None
        for hd in range(heads):
            for ahead in range(PREFETCH_HEADS * hd, min(PREFETCH_HEADS * (hd + 1), heads)):
                prefetch(ahead)
            if diag_offset is None:
                mx = max_scr[slot, hd]
            else:
                mx = score_chunk(slot, hd, 0, diag_offset)
                for c in range(1, n_chunks):
                    mx = jnp.maximum(mx, score_chunk(slot, hd, c, diag_offset))
                mx = jnp.max(mx, axis=0, keepdims=True)
            m = m_scr[hd]
            m_new = jnp.maximum(m, mx)
            m_scr[hd] = m_new
            alpha = jnp.exp2(m - m_new)
            ps = [jnp.exp2(score_chunk(slot, hd, c, diag_offset) - m_new).astype(BF16)
                  for c in range(n_chunks)]
            pv = _dot(vt_ref[j, hd * A_VROWS:(hd + 1) * A_VROWS, :n_chunks * CHUNK], jnp.concatenate(ps, axis=0))
            if pending is not None:
                acc_scr[pending[0]] = pending[1] * acc_scr[pending[0]] + pending[2]
            pending = (hd, alpha, pv)
        acc_scr[pending[0]] = pending[1] * acc_scr[pending[0]] + pending[2]

    def q_body(qi, first_buffer):
        r0 = pl.multiple_of(qi * tq, tq)
        qs = load_queries(qi)
        diag = (qi * tq) // tk
        for hd in range(heads):
            m_scr[hd] = jnp.full((1, tq), -jnp.inf, F32)
            acc_scr[hd] = jnp.zeros((A_VROWS, tq), F32)

        def body(j, _):
            for slot in range(2):
                @pl.when((j + first_buffer) % 2 == slot)
                def _():
                    tile(j, slot, lambda hd: score_head(qs, j + 1, 1 - slot, hd))
            return 0

        lax.fori_loop(0, diag, body, 0)
        diag_buffer = (diag + first_buffer) % 2
        position = (r0 - diag * tk) // tq
        for slot in range(2):
            for pos in range(tk // tq):
                @pl.when(jnp.logical_and(diag_buffer == slot, position == pos))
                def _():
                    qs_next = load_queries(jnp.minimum(qi + 1, n_query_tiles - 1))
                    tile(diag, slot, lambda hd: score_head(qs_next, 0, 1 - slot, hd), pos * tq)
                    outs = [acc_scr[hd, :A_VDIM, :] / acc_scr[hd, A_VDIM:A_VDIM + 1, :] for hd in range(heads)]
                    o_ref[pl.ds(r0, tq), :] = jnp.concatenate(outs, axis=0).T
        return 1 - diag_buffer

    qs0 = load_queries(0)
    for hd in range(heads):
        score_head(qs0, 0, 0, hd)
    lax.fori_loop(0, n_query_tiles, q_body, jnp.int32(0))


def _sb_attn_kernel(q_ref, k_ref, vt_ref, later_ref, o_ref, z_scr, later_scr, acc_scr, *, tq, tk, seq, heads):
    blk = later_ref.shape[1] // 2
    n_query_tiles = seq // tq
    lane = lax.broadcasted_iota(jnp.int32, (tq, LANES), 1)
    key = lax.broadcasted_iota(jnp.int32, (CHUNK, tq), 0)
    qry = lax.broadcasted_iota(jnp.int32, (CHUNK, tq), 1)

    def load_queries(qi):
        r0 = pl.multiple_of(qi * tq, tq)
        qs = []
        for hd in range(heads):
            pair = q_ref[pl.ds(r0, tq), (hd // 2) * LANES:(hd // 2 + 1) * LANES]
            mine = (lane < B_DIM) if hd % 2 == 0 else (lane >= B_DIM)
            qs.append(jnp.where(mine, pair, jnp.zeros_like(pair)))
        return qs

    def logits(qs, j, slot, hd):
        c0 = pl.multiple_of(j * tk, tk)
        k = k_ref[pl.ds(c0, tk), (hd // 2) * LANES:(hd // 2 + 1) * LANES]
        z_scr[slot, hd] = _dot_nt(k, qs[hd])

    def q_body(qi, _):
        r0 = pl.multiple_of(qi * tq, tq)
        qs = load_queries(qi)
        diag = (qi * tq) // tk
        qi_next = jnp.minimum(qi + 1, n_query_tiles - 1)

        def cumulative_sums(slot, hd, b, strict_limit):
            his, los = [], []
            for c in range(blk // CHUNK):
                rows = slice(b * blk + c * CHUNK, b * blk + (c + 1) * CHUNK)
                z = z_scr[slot, hd, rows, :]
                neg_abs = pltpu.bitcast(pltpu.bitcast(z, jnp.int32) | SIGN_BIT, F32)
                neg_lse = jnp.log(1.0 + jnp.exp2(neg_abs)) * (-LOG2E)
                log_rest = neg_lse - jnp.maximum(z, 0.0)
                z_scr[slot, hd, rows, :] = log_rest + z
                if strict_limit is not None:
                    log_rest = jnp.where(key + (b * blk + c * CHUNK) < strict_limit, log_rest, 0.0)
                hi = log_rest.astype(BF16)
                his.append(hi)
                los.append((log_rest - hi.astype(F32)).astype(BF16))
            return _dot(later_ref[...], jnp.concatenate(his + los, axis=0))

        def weigh_values(j, slot, hd, b, sums, later, strict_limit):
            weights = []
            for c in range(blk // CHUNK):
                rows = slice(b * blk + c * CHUNK, b * blk + (c + 1) * CHUNK)
                a = jnp.exp2(z_scr[slot, hd, rows, :] + sums[c * CHUNK:(c + 1) * CHUNK, :] + later)
                if strict_limit is not None:
                    a = jnp.where(key + (b * blk + c * CHUNK) < strict_limit, a, 0.0)
                weights.append(a.astype(BF16))
            vt = vt_ref[j, hd * B_DIM:(hd + 1) * B_DIM, b * blk:(b + 1) * blk]
            return later + sums[blk:blk + 1], _dot(vt, jnp.concatenate(weights, axis=0))

        def tile(j, slot, next_tile, next_slot, strict_limit=None, next_diagonal=None):
            blocks = list(reversed(range(tk // blk)))
            pending = None
            for hd in range(heads + 1):
                sums = [cumulative_sums(slot, hd, b, strict_limit) for b in blocks] if hd < heads else None
                if hd < heads:
                    logits(qs, next_tile, next_slot, hd)
                if pending is not None:
                    later = later_scr[hd - 1]
                    for b, block_sums in zip(blocks, pending):
                        later, weighted = weigh_values(j, slot, hd - 1, b, block_sums, later, strict_limit)
                        acc_scr[hd - 1] += weighted
                    later_scr[hd - 1] = later
                    if next_diagonal is not None:
                        logits(next_diagonal[0], next_diagonal[1], slot, hd - 1)
                pending = sums

        def any_weight_left():
            worst = later_scr[0]
            for hd in range(1, heads):
                worst = jnp.maximum(worst, later_scr[hd])
            return (jnp.max(worst) > F32_UNDERFLOW_LOG2).astype(jnp.int32)

        def more(state):
            n, live = state
            return jnp.logical_and(n <= diag, live > 0)

        def body(state):
            n, _ = state
            j = diag - n
            for slot in (1, 2):
                @pl.when(1 + (n + 1) % 2 == slot)
                def _():
                    tile(j, slot, jnp.maximum(j - 1, 0), 3 - slot)
            return n + 1, any_weight_left()

        for hd in range(heads):
            later_scr[hd] = jnp.zeros((1, tq), F32)
            acc_scr[hd] = jnp.zeros((B_DIM, tq), F32)
        tile(diag, 0, jnp.maximum(diag - 1, 0), 1, qry + (r0 - diag * tk),
             (load_queries(qi_next), (qi_next * tq) // tk))
        lax.while_loop(more, body, (jnp.int32(1), any_weight_left()))
        o_ref[pl.ds(r0, tq), :] = jnp.concatenate([acc_scr[hd] for hd in range(heads)], axis=0).T
        return 0

    qs0 = load_queries(0)
    for hd in range(heads):
        logits(qs0, 0, 0, hd)
    lax.fori_loop(0, n_query_tiles, q_body, 0)


def _attn_params():
    return pltpu.CompilerParams(dimension_semantics=("parallel", "parallel"), vmem_limit_bytes=VMEM_LIMIT)


def _mla_attn(q, k, vt, tq):
    batch, seq, _ = q.shape
    tk = vt.shape[-1]
    heads = MLA_HEADS_PER_STEP
    qk_spec = pl.BlockSpec((None, seq, heads * LANES), lambda b, g: (b, 0, g))
    return pl.pallas_call(
        functools.partial(_mla_attn_kernel, tq=tq, tk=tk, seq=seq, heads=heads),
        grid=(batch, A_HEADS // heads),
        in_specs=[qk_spec, qk_spec,
                  pl.BlockSpec((None, seq // tk, heads * A_VROWS, tk), lambda b, g: (b, 0, g, 0))],
        out_specs=pl.BlockSpec((None, seq, heads * A_VDIM), lambda b, g: (b, 0, g)),
        out_shape=jax.ShapeDtypeStruct((batch, seq, A_HEADS * A_VDIM), F32),
        scratch_shapes=[pltpu.VMEM((2, heads, tk, tq), F32), pltpu.VMEM((2, heads, 1, tq), F32),
                        pltpu.VMEM((heads, 1, tq), F32), pltpu.VMEM((heads, A_VROWS, tq), F32)],
        compiler_params=_attn_params(),
        name="mla_attn",
    )(q, k, vt)


def _sb_attn(q, k, vt, later2, tq):
    batch, seq, _ = q.shape
    tk = vt.shape[-1]
    heads = SB_HEADS_PER_STEP
    spec = pl.BlockSpec((None, seq, heads * B_DIM), lambda b, g: (b, 0, g))
    return pl.pallas_call(
        functools.partial(_sb_attn_kernel, tq=tq, tk=tk, seq=seq, heads=heads),
        grid=(batch, B_HEADS // heads),
        in_specs=[spec, spec, pl.BlockSpec((None, seq // tk, heads * B_DIM, tk), lambda b, g: (b, 0, g, 0)),
                  pl.BlockSpec(later2.shape, lambda b, g: (0, 0))],
        out_specs=spec,
        out_shape=jax.ShapeDtypeStruct((batch, seq, B_HEADS * B_DIM), F32),
        scratch_shapes=[pltpu.VMEM((3, heads, tk, tq), F32), pltpu.VMEM((heads, 1, tq), F32),
                        pltpu.VMEM((heads, B_DIM, tq), F32)],
        compiler_params=_attn_params(),
        name="sb_attn",
    )(q, k, vt, later2)


def _rope_tables(seq):
    pos = jnp.arange(seq, dtype=F32)
    inv = 1.0 / (ROPE_THETA ** (jnp.arange(0, A_ROPE, 2, dtype=F32) / A_ROPE))
    ang = pos[:, None] * inv[None, :]
    cos, sin = jnp.cos(ang), jnp.sin(ang)
    ones = jnp.ones((seq, A_NOPE), F32)
    zeros_n = jnp.zeros((seq, A_NOPE), F32)
    zeros_p = jnp.zeros((seq, LANES - A_NOPE - A_ROPE), F32)
    cos_t = jnp.concatenate([ones, cos, cos, zeros_p], axis=1)
    sin_t = jnp.concatenate([zeros_n, sin, sin, zeros_p], axis=1)
    return cos_t, sin_t


def _rot_half_cols(w):
    half = A_ROPE // 2
    return jnp.concatenate([-w[..., half:], w[..., :half]], axis=-1)


def _mla_weights(w_in, w_uq, w_ukv):
    d = w_in.shape[0]
    pad = LANES - A_NOPE - A_ROPE
    c_q = w_in[:, :A_QLORA]
    c_kv = w_in[:, A_QLORA:A_QLORA + A_KVLORA]
    k_r = w_in[:, A_QLORA + A_KVLORA:A_QLORA + A_KVLORA + A_ROPE]
    w_gate = w_in[:, A_QLORA + A_KVLORA + A_ROPE:]
    zn = jnp.zeros((d, A_NOPE), F32)
    zp = jnp.zeros((d, pad), F32)
    w_in_p = jnp.concatenate([c_q, c_kv, zn, k_r, zp, zn, _rot_half_cols(k_r), zp], axis=1)

    wq = w_uq.reshape(A_QLORA, A_HEADS, A_NOPE + A_ROPE)
    nope, rope = wq[..., :A_NOPE], wq[..., A_NOPE:]
    zqn = jnp.zeros((A_QLORA, A_HEADS, A_NOPE), F32)
    zqp = jnp.zeros((A_QLORA, A_HEADS, pad), F32)
    q_a = jnp.concatenate([nope, rope, zqp], axis=-1).reshape(A_QLORA, A_HEADS * LANES)
    q_b = jnp.concatenate([zqn, _rot_half_cols(rope), zqp], axis=-1).reshape(A_QLORA, A_HEADS * LANES)
    w_q_p = jnp.concatenate([q_a, q_b], axis=1)

    wkv = w_ukv.reshape(A_KVLORA, A_HEADS, A_NOPE + A_VDIM)
    kn, vv = wkv[..., :A_NOPE], wkv[..., A_NOPE:]
    zk = jnp.zeros((A_KVLORA, A_HEADS, LANES - A_NOPE), F32)
    w_k_p = jnp.concatenate([kn, zk], axis=-1).reshape(A_KVLORA, A_HEADS * LANES)
    w_vt = vv.reshape(A_KVLORA, A_HEADS * A_VDIM).T
    return (w_in_p.astype(BF16), w_gate.astype(BF16), w_q_p.astype(BF16), w_k_p.astype(BF16),
            w_vt.astype(BF16))


def _later_key_matrix(blk):
    s = lax.broadcasted_iota(jnp.int32, (blk, blk), 0)
    j = lax.broadcasted_iota(jnp.int32, (blk, blk), 1)
    m = jnp.concatenate([(j > s).astype(BF16), jnp.ones((BF16_SUBLANES, blk), BF16)], axis=0)
    return jnp.concatenate([m, m], axis=1)


def kernel(x, a_norm_pre, a_w_in, a_q_norm, a_w_uq, a_kv_norm, a_w_ukv, a_w_o, a_norm_post,
           b_kv_norm, b_w_kv, b_norm_pre, b_w_in, b_w_o, b_norm_post):
    batch, seq, d = x.shape
    rows = batch * seq
    tq = min(ATTN_TQ, seq)
    tk = min(MLA_TK, seq)
    x2 = x.reshape(rows, d)
    cos_t, sin_t = _rope_tables(seq)
    n_a = a_w_in.shape[0]
    n_b = b_w_in.shape[0]

    for i in range(n_a):
        w_in_p, w_gate, w_q_p, w_k_p, w_vt = _mla_weights(a_w_in[i], a_w_uq[i], a_w_ukv[i])
        q, k, vt = _mla_proj(x2, a_norm_pre[i][None], w_in_p, a_q_norm[i][None], w_q_p,
                             a_kv_norm[i][None], w_k_p, w_vt, cos_t, sin_t, seq, tk)
        o = _mla_attn(q.reshape(batch, seq, -1), k.reshape(batch, seq, -1), vt, tq)
        x2 = _out_proj(x2, o.reshape(rows, -1), a_norm_pre[i][None], w_gate,
                       a_w_o[i].astype(BF16), a_norm_post[i][None])

    width = B_HEADS * B_DIM
    tk = min(SB_TK, seq)
    later2 = _later_key_matrix(min(SB_BLOCK, tk))
    k = vt = None
    for j in range(n_b):
        w_q = b_w_in[j][:, :width].astype(BF16)
        w_gate = b_w_in[j][:, width:].astype(BF16)
        if j == 0:
            q, k, vt = _sb_proj(x2, b_norm_pre[j][None], w_q, seq, tk, b_kv_norm[None],
                                b_w_kv[:, :width].astype(BF16), b_w_kv[:, width:].T.astype(BF16))
            k = k.reshape(batch, seq, width)
        else:
            (q,) = _sb_proj(x2, b_norm_pre[j][None], w_q, seq, tk)
        o = _sb_attn(q.reshape(batch, seq, width), k, vt, later2, tq)
        x2 = _out_proj(x2, o.reshape(rows, width), b_norm_pre[j][None], w_gate,
                       b_w_o[j].astype(BF16), b_norm_post[j][None])
    return x2.reshape(batch, seq, d)
```

```python
import functools
import math

import jax
import jax.numpy as jnp
from jax import lax
from jax.experimental import pallas as pl
from jax.experimental.pallas import tpu as pltpu

D_MODEL = 1024
A_HEADS = 16
A_NOPE = 64
A_ROPE = 32
A_VDIM = 64
A_QLORA = 256
A_KVLORA = 128
B_HEADS = 16
B_DIM = 64
ROPE_THETA = 10000.0
EPS = 1e-6

LANES = 128
ROW_TILE = 512
OUT_ROW_TILE = 1024
ROW_PARTS = 2
BF16_SUBLANES = 16
A_VROWS = A_VDIM + BF16_SUBLANES
ATTN_TQ = 256
MLA_TK = 512
MLA_HEADS_PER_STEP = 4
PREFETCH_HEADS = 2
SB_TK = 256
SB_HEADS_PER_STEP = 8
SB_BLOCK = 128
SIGN_BIT = -2147483648
F32_UNDERFLOW_LOG2 = -152.0
CHUNK = 32
VMEM_LIMIT = 56 * 1024 * 1024
LOG2E = 1.4426950408889634

F32 = jnp.float32
BF16 = jnp.bfloat16


def _rms_scale(x):
    return lax.rsqrt(jnp.mean(x * x, axis=-1, keepdims=True) + EPS)


def _dot(a, b):
    return jnp.dot(a, b, preferred_element_type=F32)


def _dot_nt(a, b):
    return lax.dot_general(a, b, (((1,), (1,)), ((), ())), preferred_element_type=F32)


def _store_key_tiles(vt_ref, v_t, ones_rows=0):
    tile = vt_ref.shape[-1]
    for t in range(vt_ref.shape[0]):
        cols = slice(t * tile, (t + 1) * tile)
        if not ones_rows:
            vt_ref[t] = v_t[:, cols].astype(BF16)
            continue
        per_head = A_VDIM + ones_rows
        for hd in range(v_t.shape[0] // A_VDIM):
            vt_ref[t, hd * per_head:hd * per_head + A_VDIM, :] = v_t[hd * A_VDIM:(hd + 1) * A_VDIM, cols].astype(BF16)
            vt_ref[t, hd * per_head + A_VDIM:(hd + 1) * per_head, :] = jnp.ones((ones_rows, tile), BF16)


def _mla_proj_kernel(x_ref, g_ref, win_ref, qg_ref, wq_ref, kvg_ref, wk_ref, wvt_ref, cos_ref, sin_ref,
                     q_ref, k_ref, vt_ref):
    x = x_ref[...]
    h = x * _rms_scale(x) * g_ref[...]
    proj = _dot(h.astype(BF16), win_ref[...])
    cq = proj[:, :A_QLORA]
    ckv = proj[:, A_QLORA:A_QLORA + A_KVLORA]
    kr_a = proj[:, A_QLORA + A_KVLORA:A_QLORA + A_KVLORA + LANES]
    kr_b = proj[:, A_QLORA + A_KVLORA + LANES:]
    cqn = cq * _rms_scale(cq) * qg_ref[...]
    q2 = _dot(cqn.astype(BF16), wq_ref[...])
    ckvn = (ckv * _rms_scale(ckv) * kvg_ref[...]).astype(BF16)
    kn = _dot(ckvn, wk_ref[...])
    _store_key_tiles(vt_ref, _dot_nt(wvt_ref[...], ckvn), BF16_SUBLANES)
    cos = cos_ref[...]
    sin = sin_ref[...]
    kr = kr_a * cos + kr_b * sin
    scale = LOG2E / math.sqrt(A_NOPE + A_ROPE)
    cos_q = cos * scale
    sin_q = sin * scale
    hw = A_HEADS * LANES
    for hd in range(A_HEADS):
        sl = slice(hd * LANES, (hd + 1) * LANES)
        sl_rot = slice(hw + hd * LANES, hw + (hd + 1) * LANES)
        q_ref[:, sl] = (q2[:, sl] * cos_q + q2[:, sl_rot] * sin_q).astype(BF16)
        k_ref[:, sl] = (kn[:, sl] + kr).astype(BF16)


def _sb_proj_kernel(*refs, with_kv):
    if with_kv:
        x_ref, gq_ref, wq_ref, gkv_ref, wk_ref, wvt_ref, q_ref, k_ref, vt_ref = refs
    else:
        x_ref, gq_ref, wq_ref, q_ref = refs
    x = x_ref[...]
    y = x * _rms_scale(x)
    q = _dot((y * gq_ref[...]).astype(BF16), wq_ref[...])
    q_ref[...] = (q * (LOG2E / math.sqrt(B_DIM))).astype(BF16)
    if with_kv:
        h_kv = (y * gkv_ref[...]).astype(BF16)
        k_ref[...] = _dot(h_kv, wk_ref[...]).astype(BF16)
        _store_key_tiles(vt_ref, _dot_nt(wvt_ref[...], h_kv))


def _row_parts(tm):
    part = tm // ROW_PARTS
    return [slice(i * part, (i + 1) * part) for i in range(ROW_PARTS)]


def _out_kernel(x_ref, o_ref, gpre_ref, wg_ref, wo_ref, gpost_ref, xn_ref):
    parts = _row_parts(x_ref.shape[0])
    gates = []
    for rows in parts:
        x = x_ref[rows, :]
        h = x * _rms_scale(x) * gpre_ref[...]
        gates.append(_dot(h.astype(BF16), wg_ref[...]))
    outs = []
    for rows, gate in zip(parts, gates):
        og = o_ref[rows, :] * (gate * (1.0 / (1.0 + jnp.exp(-gate))))
        outs.append(_dot(og.astype(BF16), wo_ref[...]))
    for rows, out in zip(parts, outs):
        xn_ref[rows, :] = x_ref[rows, :] + out * _rms_scale(out) * gpost_ref[...]


def _row_spec(tm, width):
    return pl.BlockSpec((tm, width), lambda i: (i, 0))


def _full_spec(shape):
    return pl.BlockSpec(shape, lambda i: (0,) * len(shape))


def _row_params():
    return pltpu.CompilerParams(dimension_semantics=("parallel",), vmem_limit_bytes=VMEM_LIMIT)


def _vt_out(rows, seq, tm, tile, width):
    tiles_per_seq = seq // tm
    spec = pl.BlockSpec((None, tm // tile, width, tile), lambda i: (i // tiles_per_seq, i % tiles_per_seq, 0, 0))
    return spec, jax.ShapeDtypeStruct((rows // seq, seq // tile, width, tile), BF16)


def _mla_proj(x2, g, w_in_p, qg, w_q_p, kvg, w_k_p, w_vt, cos_t, sin_t, seq, tile):
    rows = x2.shape[0]
    tm = min(ROW_TILE, seq)
    tiles_per_seq = seq // tm
    hw = A_HEADS * LANES
    table_spec = pl.BlockSpec((tm, LANES), lambda i: (i % tiles_per_seq, 0))
    vt_spec, vt_shape = _vt_out(rows, seq, tm, tile, A_HEADS * A_VROWS)
    return pl.pallas_call(
        _mla_proj_kernel,
        grid=(rows // tm,),
        in_specs=[_row_spec(tm, D_MODEL), _full_spec(g.shape), _full_spec(w_in_p.shape),
                  _full_spec(qg.shape), _full_spec(w_q_p.shape), _full_spec(kvg.shape),
                  _full_spec(w_k_p.shape), _full_spec(w_vt.shape), table_spec, table_spec],
        out_specs=[_row_spec(tm, hw), _row_spec(tm, hw), vt_spec],
        out_shape=[jax.ShapeDtypeStruct((rows, hw), BF16), jax.ShapeDtypeStruct((rows, hw), BF16), vt_shape],
        compiler_params=_row_params(),
        name="mla_proj",
    )(x2, g, w_in_p, qg, w_q_p, kvg, w_k_p, w_vt, cos_t, sin_t)


def _sb_proj(x2, gq, w_q, seq, tile, gkv=None, w_k=None, w_vt=None):
    rows = x2.shape[0]
    tm = min(ROW_TILE, seq)
    width = B_HEADS * B_DIM
    with_kv = w_k is not None
    args = [x2, gq, w_q] + ([gkv, w_k, w_vt] if with_kv else [])
    in_specs = [_row_spec(tm, D_MODEL)] + [_full_spec(a.shape) for a in args[1:]]
    out_specs = [_row_spec(tm, width)]
    out_shape = [jax.ShapeDtypeStruct((rows, width), BF16)]
    if with_kv:
        vt_spec, vt_shape = _vt_out(rows, seq, tm, tile, width)
        out_specs += [_row_spec(tm, width), vt_spec]
        out_shape += [jax.ShapeDtypeStruct((rows, width), BF16), vt_shape]
    return pl.pallas_call(
        functools.partial(_sb_proj_kernel, with_kv=with_kv),
        grid=(rows // tm,),
        in_specs=in_specs,
        out_specs=out_specs,
        out_shape=out_shape,
        compiler_params=_row_params(),
        name="sb_proj_kv" if with_kv else "sb_proj",
    )(*args)


def _out_proj(x2, o2, gpre, w_gate, w_o, gpost):
    rows = x2.shape[0]
    tm = min(OUT_ROW_TILE, rows)
    return pl.pallas_call(
        _out_kernel,
        grid=(rows // tm,),
        in_specs=[_row_spec(tm, D_MODEL), _row_spec(tm, o2.shape[1]), _full_spec(gpre.shape),
                  _full_spec(w_gate.shape), _full_spec(w_o.shape), _full_spec(gpost.shape)],
        out_specs=_row_spec(tm, D_MODEL),
        out_shape=jax.ShapeDtypeStruct((rows, D_MODEL), F32),
        compiler_params=_row_params(),
        name="out_proj",
    )(x2, o2, gpre, w_gate, w_o, gpost)


def _mla_attn_kernel(q_ref, k_ref, vt_ref, o_ref, s_scr, max_scr, m_scr, acc_scr, *, tq, tk, seq, heads):
    n_query_tiles = seq // tq
    key = lax.broadcasted_iota(jnp.int32, (CHUNK, tq), 0)
    qry = lax.broadcasted_iota(jnp.int32, (CHUNK, tq), 1)

    def load_queries(qi):
        r0 = pl.multiple_of(qi * tq, tq)
        return [q_ref[pl.ds(r0, tq), hd * LANES:(hd + 1) * LANES] for hd in range(heads)]

    def score_head(qs, j, slot, hd):
        c0 = pl.multiple_of(j * tk, tk)
        s = _dot_nt(k_ref[pl.ds(c0, tk), hd * LANES:(hd + 1) * LANES], qs[hd])
        s_scr[slot, hd] = s
        max_scr[slot, hd] = jnp.max(s, axis=0, keepdims=True)

    def score_chunk(slot, hd, c, diag_offset):
        blk = s_scr[slot, hd, c * CHUNK:(c + 1) * CHUNK, :]
        if diag_offset is None or (c + 1) * CHUNK - 1 <= diag_offset:
            return blk
        return jnp.where(key + (c * CHUNK - diag_offset) <= qry, blk, -jnp.inf)

    def tile(j, slot, prefetch, diag_offset=None):
        n_chunks = tk // CHUNK if diag_offset is None else (diag_offset + tq) // CHUNK
        pending = None
        for hd in range(heads):
            for ahead in range(PREFETCH_HEADS * hd, min(PREFETCH_HEADS * (hd + 1), heads)):
                prefetch(ahead)
            if diag_offset is None:
                mx = max_scr[slot, hd]
            else:
                mx = score_chunk(slot, hd, 0, diag_offset)
                for c in range(1, n_chunks):
                    mx = jnp.maximum(mx, score_chunk(slot, hd, c, diag_offset))
                mx = jnp.max(mx, axis=0, keepdims=True)
            m = m_scr[hd]
            m_new = jnp.maximum(m, mx)
            m_scr[hd] = m_new
            alpha = jnp.exp2(m - m_new)
            ps = [jnp.exp2(score_chunk(slot, hd, c, diag_offset) - m_new).astype(BF16)
                  for c in range(n_chunks)]
            pv = _dot(vt_ref[j, hd * A_VROWS:(hd + 1) * A_VROWS, :n_chunks * CHUNK], jnp.concatenate(ps, axis=0))
            if pending is not None:
                acc_scr[pending[0]] = pending[1] * acc_scr[pending[0]] + pending[2]
            pending = (hd, alpha, pv)
        acc_scr[pending[0]] = pending[1] * acc_scr[pending[0]] + pending[2]

    def q_body(qi, first_buffer):
        r0 = pl.multiple_of(qi * tq, tq)
        qs = load_queries(qi)
        diag = (qi * tq) // tk
        for hd in range(heads):
            m_scr[hd] = jnp.full((1, tq), -jnp.inf, F32)
            acc_scr[hd] = jnp.zeros((A_VROWS, tq), F32)

        def body(j, _):
            for slot in range(2):
                @pl.when((j + first_buffer) % 2 == slot)
                def _():
                    tile(j, slot, lambda hd: score_head(qs, j + 1, 1 - slot, hd))
            return 0

        lax.fori_loop(0, diag, body, 0)
        diag_buffer = (diag + first_buffer) % 2
        position = (r0 - diag * tk) // tq
        for slot in range(2):
            for pos in range(tk // tq):
                @pl.when(jnp.logical_and(diag_buffer == slot, position == pos))
                def _():
                    qs_next = load_queries(jnp.minimum(qi + 1, n_query_tiles - 1))
                    tile(diag, slot, lambda hd: score_head(qs_next, 0, 1 - slot, hd), pos * tq)
                    outs = [acc_scr[hd, :A_VDIM, :] / acc_scr[hd, A_VDIM:A_VDIM + 1, :] for hd in range(heads)]
                    o_ref[pl.ds(r0, tq), :] = jnp.concatenate(outs, axis=0).T
        return 1 - diag_buffer

    qs0 = load_queries(0)
    for hd in range(heads):
        score_head(qs0, 0, 0, hd)
    lax.fori_loop(0, n_query_tiles, q_body, jnp.int32(0))


def _sb_attn_kernel(q_ref, k_ref, vt_ref, later_ref, o_ref, z_scr, later_scr, acc_scr, *, tq, tk, seq, heads):
    blk = later_ref.shape[1]
    lane = lax.broadcasted_iota(jnp.int32, (tq, LANES), 1)
    key = lax.broadcasted_iota(jnp.int32, (CHUNK, tq), 0)
    qry = lax.broadcasted_iota(jnp.int32, (CHUNK, tq), 1)

    def load_queries(qi):
        r0 = pl.multiple_of(qi * tq, tq)
        qs = []
        for hd in range(heads):
            pair = q_ref[pl.ds(r0, tq), (hd // 2) * LANES:(hd // 2 + 1) * LANES]
            mine = (lane < B_DIM) if hd % 2 == 0 else (lane >= B_DIM)
            qs.append(jnp.where(mine, pair, jnp.zeros_like(pair)))
        return qs

    def logits(qs, j, slot, hd):
        c0 = pl.multiple_of(j * tk, tk)
        k = k_ref[pl.ds(c0, tk), (hd // 2) * LANES:(hd // 2 + 1) * LANES]
        z_scr[slot, hd] = _dot_nt(k, qs[hd])

    def q_body(qi, _):
        r0 = pl.multiple_of(qi * tq, tq)
        qs = load_queries(qi)
        diag = (qi * tq) // tk

        def cumulative_sums(slot, hd, b, strict_limit):
            parts = []
            for c in range(blk // CHUNK):
                rows = slice(b * blk + c * CHUNK, b * blk + (c + 1) * CHUNK)
                z = z_scr[slot, hd, rows, :]
                neg_abs = pltpu.bitcast(pltpu.bitcast(z, jnp.int32) | SIGN_BIT, F32)
                neg_lse = jnp.log(1.0 + jnp.exp2(neg_abs)) * (-LOG2E)
                log_rest = neg_lse - jnp.maximum(z, 0.0)
                z_scr[slot, hd, rows, :] = log_rest + z
                if strict_limit is not None:
                    log_rest = jnp.where(key + (b * blk + c * CHUNK) < strict_limit, log_rest, 0.0)
                parts.append(log_rest.astype(BF16))
            return _dot(later_ref[...], jnp.concatenate(parts, axis=0))

        def weigh_values(j, slot, hd, b, sums, later, strict_limit):
            weights = []
            for c in range(blk // CHUNK):
                rows = slice(b * blk + c * CHUNK, b * blk + (c + 1) * CHUNK)
                a = jnp.exp2(z_scr[slot, hd, rows, :] + sums[c * CHUNK:(c + 1) * CHUNK, :] + later)
                if strict_limit is not None:
                    a = jnp.where(key + (b * blk + c * CHUNK) < strict_limit, a, 0.0)
                weights.append(a.astype(BF16))
            vt = vt_ref[j, hd * B_DIM:(hd + 1) * B_DIM, b * blk:(b + 1) * blk]
            return later + sums[blk:blk + 1], _dot(vt, jnp.concatenate(weights, axis=0))

        def tile(j, slot, next_tile, strict_limit=None):
            blocks = list(reversed(range(tk // blk)))
            pending = None
            for hd in range(heads + 1):
                sums = [cumulative_sums(slot, hd, b, strict_limit) for b in blocks] if hd < heads else None
                if hd < heads:
                    logits(qs, next_tile, 1 - slot, hd)
                if pending is not None:
                    later = later_scr[hd - 1]
                    for b, block_sums in zip(blocks, pending):
                        later, weighted = weigh_values(j, slot, hd - 1, b, block_sums, later, strict_limit)
                        acc_scr[hd - 1] += weighted
                    later_scr[hd - 1] = later
                pending = sums

        def any_weight_left():
            worst = later_scr[0]
            for hd in range(1, heads):
                worst = jnp.maximum(worst, later_scr[hd])
            return (jnp.max(worst) > F32_UNDERFLOW_LOG2).astype(jnp.int32)

        def more(state):
            n, live = state
            return jnp.logical_and(n <= diag, live > 0)

        def body(state):
            n, _ = state
            j = diag - n
            for slot in range(2):
                @pl.when(n % 2 == slot)
                def _():
                    tile(j, slot, jnp.maximum(j - 1, 0))
            return n + 1, any_weight_left()

        for hd in range(heads):
            logits(qs, diag, 0, hd)
            later_scr[hd] = jnp.zeros((1, tq), F32)
            acc_scr[hd] = jnp.zeros((B_DIM, tq), F32)
        tile(diag, 0, jnp.maximum(diag - 1, 0), qry + (r0 - diag * tk))
        lax.while_loop(more, body, (jnp.int32(1), any_weight_left()))
        o_ref[pl.ds(r0, tq), :] = jnp.concatenate([acc_scr[hd] for hd in range(heads)], axis=0).T
        return 0

    lax.fori_loop(0, seq // tq, q_body, 0)


def _attn_params():
    return pltpu.CompilerParams(dimension_semantics=("parallel", "parallel"), vmem_limit_bytes=VMEM_LIMIT)


def _mla_attn(q, k, vt, tq):
    batch, seq, _ = q.shape
    tk = vt.shape[-1]
    heads = MLA_HEADS_PER_STEP
    qk_spec = pl.BlockSpec((None, seq, heads * LANES), lambda b, g: (b, 0, g))
    return pl.pallas_call(
        functools.partial(_mla_attn_kernel, tq=tq, tk=tk, seq=seq, heads=heads),
        grid=(batch, A_HEADS // heads),
        in_specs=[qk_spec, qk_spec,
                  pl.BlockSpec((None, seq // tk, heads * A_VROWS, tk), lambda b, g: (b, 0, g, 0))],
        out_specs=pl.BlockSpec((None, seq, heads * A_VDIM), lambda b, g: (b, 0, g)),
        out_shape=jax.ShapeDtypeStruct((batch, seq, A_HEADS * A_VDIM), F32),
        scratch_shapes=[pltpu.VMEM((2, heads, tk, tq), F32), pltpu.VMEM((2, heads, 1, tq), F32),
                        pltpu.VMEM((heads, 1, tq), F32), pltpu.VMEM((heads, A_VROWS, tq), F32)],
        compiler_params=_attn_params(),
        name="mla_attn",
    )(q, k, vt)


def _sb_attn(q, k, vt, later2, tq):
    batch, seq, _ = q.shape
    tk = vt.shape[-1]
    heads = SB_HEADS_PER_STEP
    spec = pl.BlockSpec((None, seq, heads * B_DIM), lambda b, g: (b, 0, g))
    return pl.pallas_call(
        functools.partial(_sb_attn_kernel, tq=tq, tk=tk, seq=seq, heads=heads),
        grid=(batch, B_HEADS // heads),
        in_specs=[spec, spec, pl.BlockSpec((None, seq // tk, heads * B_DIM, tk), lambda b, g: (b, 0, g, 0)),
                  pl.BlockSpec(later2.shape, lambda b, g: (0, 0))],
        out_specs=spec,
        out_shape=jax.ShapeDtypeStruct((batch, seq, B_HEADS * B_DIM), F32),
        scratch_shapes=[pltpu.VMEM((2, heads, tk, tq), F32), pltpu.VMEM((heads, 1, tq), F32),
                        pltpu.VMEM((heads, B_DIM, tq), F32)],
        compiler_params=_attn_params(),
        name="sb_attn",
    )(q, k, vt, later2)


def _rope_tables(seq):
    pos = jnp.arange(seq, dtype=F32)
    inv = 1.0 / (ROPE_THETA ** (jnp.arange(0, A_ROPE, 2, dtype=F32) / A_ROPE))
    ang = pos[:, None] * inv[None, :]
    cos, sin = jnp.cos(ang), jnp.sin(ang)
    ones = jnp.ones((seq, A_NOPE), F32)
    zeros_n = jnp.zeros((seq, A_NOPE), F32)
    zeros_p = jnp.zeros((seq, LANES - A_NOPE - A_ROPE), F32)
    cos_t = jnp.concatenate([ones, cos, cos, zeros_p], axis=1)
    sin_t = jnp.concatenate([zeros_n, sin, sin, zeros_p], axis=1)
    return cos_t, sin_t


def _rot_half_cols(w):
    half = A_ROPE // 2
    return jnp.concatenate([-w[..., half:], w[..., :half]], axis=-1)


def _mla_weights(w_in, w_uq, w_ukv):
    d = w_in.shape[0]
    pad = LANES - A_NOPE - A_ROPE
    c_q = w_in[:, :A_QLORA]
    c_kv = w_in[:, A_QLORA:A_QLORA + A_KVLORA]
    k_r = w_in[:, A_QLORA + A_KVLORA:A_QLORA + A_KVLORA + A_ROPE]
    w_gate = w_in[:, A_QLORA + A_KVLORA + A_ROPE:]
    zn = jnp.zeros((d, A_NOPE), F32)
    zp = jnp.zeros((d, pad), F32)
    w_in_p = jnp.concatenate([c_q, c_kv, zn, k_r, zp, zn, _rot_half_cols(k_r), zp], axis=1)

    wq = w_uq.reshape(A_QLORA, A_HEADS, A_NOPE + A_ROPE)
    nope, rope = wq[..., :A_NOPE], wq[..., A_NOPE:]
    zqn = jnp.zeros((A_QLORA, A_HEADS, A_NOPE), F32)
    zqp = jnp.zeros((A_QLORA, A_HEADS, pad), F32)
    q_a = jnp.concatenate([nope, rope, zqp], axis=-1).reshape(A_QLORA, A_HEADS * LANES)
    q_b = jnp.concatenate([zqn, _rot_half_cols(rope), zqp], axis=-1).reshape(A_QLORA, A_HEADS * LANES)
    w_q_p = jnp.concatenate([q_a, q_b], axis=1)

    wkv = w_ukv.reshape(A_KVLORA, A_HEADS, A_NOPE + A_VDIM)
    kn, vv = wkv[..., :A_NOPE], wkv[..., A_NOPE:]
    zk = jnp.zeros((A_KVLORA, A_HEADS, LANES - A_NOPE), F32)
    w_k_p = jnp.concatenate([kn, zk], axis=-1).reshape(A_KVLORA, A_HEADS * LANES)
    w_vt = vv.reshape(A_KVLORA, A_HEADS * A_VDIM).T
    return (w_in_p.astype(BF16), w_gate.astype(BF16), w_q_p.astype(BF16), w_k_p.astype(BF16),
            w_vt.astype(BF16))


def _later_key_matrix(blk):
    s = lax.broadcasted_iota(jnp.int32, (blk, blk), 0)
    j = lax.broadcasted_iota(jnp.int32, (blk, blk), 1)
    return jnp.concatenate([(j > s).astype(BF16), jnp.ones((BF16_SUBLANES, blk), BF16)], axis=0)


def kernel(x, a_norm_pre, a_w_in, a_q_norm, a_w_uq, a_kv_norm, a_w_ukv, a_w_o, a_norm_post,
           b_kv_norm, b_w_kv, b_norm_pre, b_w_in, b_w_o, b_norm_post):
    batch, seq, d = x.shape
    rows = batch * seq
    tq = min(ATTN_TQ, seq)
    tk = min(MLA_TK, seq)
    x2 = x.reshape(rows, d)
    cos_t, sin_t = _rope_tables(seq)
    n_a = a_w_in.shape[0]
    n_b = b_w_in.shape[0]

    for i in range(n_a):
        w_in_p, w_gate, w_q_p, w_k_p, w_vt = _mla_weights(a_w_in[i], a_w_uq[i], a_w_ukv[i])
        q, k, vt = _mla_proj(x2, a_norm_pre[i][None], w_in_p, a_q_norm[i][None], w_q_p,
                             a_kv_norm[i][None], w_k_p, w_vt, cos_t, sin_t, seq, tk)
        o = _mla_attn(q.reshape(batch, seq, -1), k.reshape(batch, seq, -1), vt, tq)
        x2 = _out_proj(x2, o.reshape(rows, -1), a_norm_pre[i][None], w_gate,
                       a_w_o[i].astype(BF16), a_norm_post[i][None])

    width = B_HEADS * B_DIM
    tk = min(SB_TK, seq)
    later2 = _later_key_matrix(min(SB_BLOCK, tk))
    k = vt = None
    for j in range(n_b):
        w_q = b_w_in[j][:, :width].astype(BF16)
        w_gate = b_w_in[j][:, width:].astype(BF16)
        if j == 0:
            q, k, vt = _sb_proj(x2, b_norm_pre[j][None], w_q, seq, tk, b_kv_norm[None],
                                b_w_kv[:, :width].astype(BF16), b_w_kv[:, width:].T.astype(BF16))
            k = k.reshape(batch, seq, width)
        else:
            (q,) = _sb_proj(x2, b_norm_pre[j][None], w_q, seq, tk)
        o = _sb_attn(q.reshape(batch, seq, width), k, vt, later2, tq)
        x2 = _out_proj(x2, o.reshape(rows, width), b_norm_pre[j][None], w_gate,
                       b_w_o[j].astype(BF16), b_norm_post[j][None])
    return x2.reshape(batch, seq, d)
```

```python
import functools
import math

import jax
import jax.numpy as jnp
from jax import lax
from jax.experimental import pallas as pl
from jax.experimental.pallas import tpu as pltpu

D_MODEL = 1024
A_HEADS = 16
A_NOPE = 64
A_ROPE = 32
A_VDIM = 64
A_QLORA = 256
A_KVLORA = 128
B_HEADS = 16
B_DIM = 64
ROPE_THETA = 10000.0
EPS = 1e-6

LANES = 128
ROW_TILE = 512
OUT_ROW_TILE = 1024
ROW_PARTS = 2
BF16_SUBLANES = 16
A_VROWS = A_VDIM + BF16_SUBLANES
ATTN_TQ = 256
MLA_TK = 512
MLA_HEADS_PER_STEP = 4
PREFETCH_HEADS = 2
SB_TK = 256
SB_HEADS_PER_STEP = 8
SB_BLOCK = 128
F32_UNDERFLOW_LOG2 = -152.0
CHUNK = 32
VMEM_LIMIT = 56 * 1024 * 1024
LOG2E = 1.4426950408889634

F32 = jnp.float32
BF16 = jnp.bfloat16


def _rms_scale(x):
    return lax.rsqrt(jnp.mean(x * x, axis=-1, keepdims=True) + EPS)


def _dot(a, b):
    return jnp.dot(a, b, preferred_element_type=F32)


def _dot_nt(a, b):
    return lax.dot_general(a, b, (((1,), (1,)), ((), ())), preferred_element_type=F32)


def _store_key_tiles(vt_ref, v_t, ones_rows=0):
    tile = vt_ref.shape[-1]
    for t in range(vt_ref.shape[0]):
        cols = slice(t * tile, (t + 1) * tile)
        if not ones_rows:
            vt_ref[t] = v_t[:, cols].astype(BF16)
            continue
        per_head = A_VDIM + ones_rows
        for hd in range(v_t.shape[0] // A_VDIM):
            vt_ref[t, hd * per_head:hd * per_head + A_VDIM, :] = v_t[hd * A_VDIM:(hd + 1) * A_VDIM, cols].astype(BF16)
            vt_ref[t, hd * per_head + A_VDIM:(hd + 1) * per_head, :] = jnp.ones((ones_rows, tile), BF16)


def _mla_proj_kernel(x_ref, g_ref, win_ref, qg_ref, wq_ref, kvg_ref, wk_ref, wvt_ref, cos_ref, sin_ref,
                     q_ref, k_ref, vt_ref):
    x = x_ref[...]
    h = x * _rms_scale(x) * g_ref[...]
    proj = _dot(h.astype(BF16), win_ref[...])
    cq = proj[:, :A_QLORA]
    ckv = proj[:, A_QLORA:A_QLORA + A_KVLORA]
    kr_a = proj[:, A_QLORA + A_KVLORA:A_QLORA + A_KVLORA + LANES]
    kr_b = proj[:, A_QLORA + A_KVLORA + LANES:]
    cqn = cq * _rms_scale(cq) * qg_ref[...]
    q2 = _dot(cqn.astype(BF16), wq_ref[...])
    ckvn = (ckv * _rms_scale(ckv) * kvg_ref[...]).astype(BF16)
    kn = _dot(ckvn, wk_ref[...])
    _store_key_tiles(vt_ref, _dot_nt(wvt_ref[...], ckvn), BF16_SUBLANES)
    cos = cos_ref[...]
    sin = sin_ref[...]
    kr = kr_a * cos + kr_b * sin
    scale = LOG2E / math.sqrt(A_NOPE + A_ROPE)
    cos_q = cos * scale
    sin_q = sin * scale
    hw = A_HEADS * LANES
    for hd in range(A_HEADS):
        sl = slice(hd * LANES, (hd + 1) * LANES)
        sl_rot = slice(hw + hd * LANES, hw + (hd + 1) * LANES)
        q_ref[:, sl] = (q2[:, sl] * cos_q + q2[:, sl_rot] * sin_q).astype(BF16)
        k_ref[:, sl] = (kn[:, sl] + kr).astype(BF16)


def _sb_proj_kernel(*refs, with_kv):
    if with_kv:
        x_ref, gq_ref, wq_ref, gkv_ref, wk_ref, wvt_ref, q_ref, k_ref, vt_ref = refs
    else:
        x_ref, gq_ref, wq_ref, q_ref = refs
    x = x_ref[...]
    y = x * _rms_scale(x)
    q = _dot((y * gq_ref[...]).astype(BF16), wq_ref[...])
    q_ref[...] = (q * (LOG2E / math.sqrt(B_DIM))).astype(BF16)
    if with_kv:
        h_kv = (y * gkv_ref[...]).astype(BF16)
        k_ref[...] = _dot(h_kv, wk_ref[...]).astype(BF16)
        _store_key_tiles(vt_ref, _dot_nt(wvt_ref[...], h_kv))


def _row_parts(tm):
    part = tm // ROW_PARTS
    return [slice(i * part, (i + 1) * part) for i in range(ROW_PARTS)]


def _out_kernel(x_ref, o_ref, gpre_ref, wg_ref, wo_ref, gpost_ref, xn_ref):
    parts = _row_parts(x_ref.shape[0])
    gates = []
    for rows in parts:
        x = x_ref[rows, :]
        h = x * _rms_scale(x) * gpre_ref[...]
        gates.append(_dot(h.astype(BF16), wg_ref[...]))
    outs = []
    for rows, gate in zip(parts, gates):
        og = o_ref[rows, :] * (gate * (1.0 / (1.0 + jnp.exp(-gate))))
        outs.append(_dot(og.astype(BF16), wo_ref[...]))
    for rows, out in zip(parts, outs):
        xn_ref[rows, :] = x_ref[rows, :] + out * _rms_scale(out) * gpost_ref[...]


def _row_spec(tm, width):
    return pl.BlockSpec((tm, width), lambda i: (i, 0))


def _full_spec(shape):
    return pl.BlockSpec(shape, lambda i: (0,) * len(shape))


def _row_params():
    return pltpu.CompilerParams(dimension_semantics=("parallel",), vmem_limit_bytes=VMEM_LIMIT)


def _vt_out(rows, seq, tm, tile, width):
    tiles_per_seq = seq // tm
    spec = pl.BlockSpec((None, tm // tile, width, tile), lambda i: (i // tiles_per_seq, i % tiles_per_seq, 0, 0))
    return spec, jax.ShapeDtypeStruct((rows // seq, seq // tile, width, tile), BF16)


def _mla_proj(x2, g, w_in_p, qg, w_q_p, kvg, w_k_p, w_vt, cos_t, sin_t, seq, tile):
    rows = x2.shape[0]
    tm = min(ROW_TILE, seq)
    tiles_per_seq = seq // tm
    hw = A_HEADS * LANES
    table_spec = pl.BlockSpec((tm, LANES), lambda i: (i % tiles_per_seq, 0))
    vt_spec, vt_shape = _vt_out(rows, seq, tm, tile, A_HEADS * A_VROWS)
    return pl.pallas_call(
        _mla_proj_kernel,
        grid=(rows // tm,),
        in_specs=[_row_spec(tm, D_MODEL), _full_spec(g.shape), _full_spec(w_in_p.shape),
                  _full_spec(qg.shape), _full_spec(w_q_p.shape), _full_spec(kvg.shape),
                  _full_spec(w_k_p.shape), _full_spec(w_vt.shape), table_spec, table_spec],
        out_specs=[_row_spec(tm, hw), _row_spec(tm, hw), vt_spec],
        out_shape=[jax.ShapeDtypeStruct((rows, hw), BF16), jax.ShapeDtypeStruct((rows, hw), BF16), vt_shape],
        compiler_params=_row_params(),
        name="mla_proj",
    )(x2, g, w_in_p, qg, w_q_p, kvg, w_k_p, w_vt, cos_t, sin_t)


def _sb_proj(x2, gq, w_q, seq, tile, gkv=None, w_k=None, w_vt=None):
    rows = x2.shape[0]
    tm = min(ROW_TILE, seq)
    width = B_HEADS * B_DIM
    with_kv = w_k is not None
    args = [x2, gq, w_q] + ([gkv, w_k, w_vt] if with_kv else [])
    in_specs = [_row_spec(tm, D_MODEL)] + [_full_spec(a.shape) for a in args[1:]]
    out_specs = [_row_spec(tm, width)]
    out_shape = [jax.ShapeDtypeStruct((rows, width), BF16)]
    if with_kv:
        vt_spec, vt_shape = _vt_out(rows, seq, tm, tile, width)
        out_specs += [_row_spec(tm, width), vt_spec]
        out_shape += [jax.ShapeDtypeStruct((rows, width), BF16), vt_shape]
    return pl.pallas_call(
        functools.partial(_sb_proj_kernel, with_kv=with_kv),
        grid=(rows // tm,),
        in_specs=in_specs,
        out_specs=out_specs,
        out_shape=out_shape,
        compiler_params=_row_params(),
        name="sb_proj_kv" if with_kv else "sb_proj",
    )(*args)


def _out_proj(x2, o2, gpre, w_gate, w_o, gpost):
    rows = x2.shape[0]
    tm = min(OUT_ROW_TILE, rows)
    return pl.pallas_call(
        _out_kernel,
        grid=(rows // tm,),
        in_specs=[_row_spec(tm, D_MODEL), _row_spec(tm, o2.shape[1]), _full_spec(gpre.shape),
                  _full_spec(w_gate.shape), _full_spec(w_o.shape), _full_spec(gpost.shape)],
        out_specs=_row_spec(tm, D_MODEL),
        out_shape=jax.ShapeDtypeStruct((rows, D_MODEL), F32),
        compiler_params=_row_params(),
        name="out_proj",
    )(x2, o2, gpre, w_gate, w_o, gpost)


def _mla_attn_kernel(q_ref, k_ref, vt_ref, o_ref, s_scr, max_scr, m_scr, acc_scr, *, tq, tk, seq, heads):
    n_query_tiles = seq // tq
    key = lax.broadcasted_iota(jnp.int32, (CHUNK, tq), 0)
    qry = lax.broadcasted_iota(jnp.int32, (CHUNK, tq), 1)

    def load_queries(qi):
        r0 = pl.multiple_of(qi * tq, tq)
        return [q_ref[pl.ds(r0, tq), hd * LANES:(hd + 1) * LANES] for hd in range(heads)]

    def score_head(qs, j, slot, hd):
        c0 = pl.multiple_of(j * tk, tk)
        s = _dot_nt(k_ref[pl.ds(c0, tk), hd * LANES:(hd + 1) * LANES], qs[hd])
        s_scr[slot, hd] = s
        max_scr[slot, hd] = jnp.max(s, axis=0, keepdims=True)

    def score_chunk(slot, hd, c, diag_offset):
        blk = s_scr[slot, hd, c * CHUNK:(c + 1) * CHUNK, :]
        if diag_offset is None or (c + 1) * CHUNK - 1 <= diag_offset:
            return blk
        return jnp.where(key + (c * CHUNK - diag_offset) <= qry, blk, -jnp.inf)

    def tile(j, slot, prefetch, diag_offset=None):
        n_chunks = tk // CHUNK if diag_offset is None else (diag_offset + tq) // CHUNK
        pending = None
        for hd in range(heads):
            for ahead in range(PREFETCH_HEADS * hd, min(PREFETCH_HEADS * (hd + 1), heads)):
                prefetch(ahead)
            if diag_offset is None:
                mx = max_scr[slot, hd]
            else:
                mx = score_chunk(slot, hd, 0, diag_offset)
                for c in range(1, n_chunks):
                    mx = jnp.maximum(mx, score_chunk(slot, hd, c, diag_offset))
                mx = jnp.max(mx, axis=0, keepdims=True)
            m = m_scr[hd]
            m_new = jnp.maximum(m, mx)
            m_scr[hd] = m_new
            alpha = jnp.exp2(m - m_new)
            ps = [jnp.exp2(score_chunk(slot, hd, c, diag_offset) - m_new).astype(BF16)
                  for c in range(n_chunks)]
            pv = _dot(vt_ref[j, hd * A_VROWS:(hd + 1) * A_VROWS, :n_chunks * CHUNK], jnp.concatenate(ps, axis=0))
            if pending is not None:
                acc_scr[pending[0]] = pending[1] * acc_scr[pending[0]] + pending[2]
            pending = (hd, alpha, pv)
        acc_scr[pending[0]] = pending[1] * acc_scr[pending[0]] + pending[2]

    def q_body(qi, first_buffer):
        r0 = pl.multiple_of(qi * tq, tq)
        qs = load_queries(qi)
        diag = (qi * tq) // tk
        for hd in range(heads):
            m_scr[hd] = jnp.full((1, tq), -jnp.inf, F32)
            acc_scr[hd] = jnp.zeros((A_VROWS, tq), F32)

        def body(j, _):
            for slot in range(2):
                @pl.when((j + first_buffer) % 2 == slot)
                def _():
                    tile(j, slot, lambda hd: score_head(qs, j + 1, 1 - slot, hd))
            return 0

        lax.fori_loop(0, diag, body, 0)
        diag_buffer = (diag + first_buffer) % 2
        position = (r0 - diag * tk) // tq
        for slot in range(2):
            for pos in range(tk // tq):
                @pl.when(jnp.logical_and(diag_buffer == slot, position == pos))
                def _():
                    qs_next = load_queries(jnp.minimum(qi + 1, n_query_tiles - 1))
                    tile(diag, slot, lambda hd: score_head(qs_next, 0, 1 - slot, hd), pos * tq)
                    outs = [acc_scr[hd, :A_VDIM, :] / acc_scr[hd, A_VDIM:A_VDIM + 1, :] for hd in range(heads)]
                    o_ref[pl.ds(r0, tq), :] = jnp.concatenate(outs, axis=0).T
        return 1 - diag_buffer

    qs0 = load_queries(0)
    for hd in range(heads):
        score_head(qs0, 0, 0, hd)
    lax.fori_loop(0, n_query_tiles, q_body, jnp.int32(0))


def _sb_attn_kernel(q_ref, k_ref, vt_ref, later_ref, o_ref, z_scr, later_scr, acc_scr, *, tq, tk, seq, heads):
    blk = later_ref.shape[1]
    n_query_tiles = seq // tq
    lane = lax.broadcasted_iota(jnp.int32, (tq, LANES), 1)
    key = lax.broadcasted_iota(jnp.int32, (CHUNK, tq), 0)
    qry = lax.broadcasted_iota(jnp.int32, (CHUNK, tq), 1)

    def load_queries(qi):
        r0 = pl.multiple_of(qi * tq, tq)
        qs = []
        for hd in range(heads):
            pair = q_ref[pl.ds(r0, tq), (hd // 2) * LANES:(hd // 2 + 1) * LANES]
            mine = (lane < B_DIM) if hd % 2 == 0 else (lane >= B_DIM)
            qs.append(jnp.where(mine, pair, jnp.zeros_like(pair)))
        return qs

    def logits(qs, j, slot, hd):
        c0 = pl.multiple_of(j * tk, tk)
        k = k_ref[pl.ds(c0, tk), (hd // 2) * LANES:(hd // 2 + 1) * LANES]
        z_scr[slot, hd] = _dot_nt(k, qs[hd])

    def q_body(qi, _):
        r0 = pl.multiple_of(qi * tq, tq)
        qs = load_queries(qi)
        diag = (qi * tq) // tk

        def cumulative_sums(slot, hd, b, strict_limit):
            parts = []
            for c in range(blk // CHUNK):
                rows = slice(b * blk + c * CHUNK, b * blk + (c + 1) * CHUNK)
                z = z_scr[slot, hd, rows, :]
                neg_lse = jnp.log(1.0 + jnp.exp2(-jnp.abs(z))) * (-LOG2E)
                log_rest = neg_lse - jnp.maximum(z, 0.0)
                z_scr[slot, hd, rows, :] = log_rest + z
                if strict_limit is not None:
                    log_rest = jnp.where(key + (b * blk + c * CHUNK) < strict_limit, log_rest, 0.0)
                parts.append(log_rest.astype(BF16))
            return _dot(later_ref[...], jnp.concatenate(parts, axis=0))

        def weigh_values(j, slot, hd, b, sums, later, strict_limit):
            weights = []
            for c in range(blk // CHUNK):
                rows = slice(b * blk + c * CHUNK, b * blk + (c + 1) * CHUNK)
                a = jnp.exp2(z_scr[slot, hd, rows, :] + sums[c * CHUNK:(c + 1) * CHUNK, :] + later)
                if strict_limit is not None:
                    a = jnp.where(key + (b * blk + c * CHUNK) < strict_limit, a, 0.0)
                weights.append(a.astype(BF16))
            vt = vt_ref[j, hd * B_DIM:(hd + 1) * B_DIM, b * blk:(b + 1) * blk]
            return later + sums[blk:blk + 1], _dot(vt, jnp.concatenate(weights, axis=0))

        def tile(j, slot, strict_limit=None, ahead=None, recycle=None):
            blocks = list(reversed(range(tk // blk)))
            pending = None
            for hd in range(heads + 1):
                sums = [cumulative_sums(slot, hd, b, strict_limit) for b in blocks] if hd < heads else None
                if ahead is not None and hd < heads:
                    logits(qs, ahead[0], ahead[1], hd)
                if pending is not None:
                    later = later_scr[hd - 1]
                    for b, block_sums in zip(blocks, pending):
                        later, weighted = weigh_values(j, slot, hd - 1, b, block_sums, later, strict_limit)
                        acc_scr[hd - 1] += weighted
                    later_scr[hd - 1] = later
                    if recycle is not None:
                        logits(recycle[0], recycle[1], slot, hd - 1)
                pending = sums

        def any_weight_left():
            worst = later_scr[0]
            for hd in range(1, heads):
                worst = jnp.maximum(worst, later_scr[hd])
            return (jnp.max(worst) > F32_UNDERFLOW_LOG2).astype(jnp.int32)

        def more(state):
            n, live = state
            return jnp.logical_and(n <= diag, live > 0)

        def body(state):
            n, _ = state
            j = diag - n

            @pl.when(n >= 2)
            def _():
                for hd in range(heads):
                    logits(qs, j, 1, hd)

            tile(j, 1)
            return n + 1, any_weight_left()

        for hd in range(heads):
            later_scr[hd] = jnp.zeros((1, tq), F32)
            acc_scr[hd] = jnp.zeros((B_DIM, tq), F32)
        qi_next = jnp.minimum(qi + 1, n_query_tiles - 1)
        tile(diag, 0, qry + (r0 - diag * tk), ahead=(jnp.maximum(diag - 1, 0), 1),
             recycle=(load_queries(qi_next), (qi_next * tq) // tk))
        lax.while_loop(more, body, (jnp.int32(1), any_weight_left()))
        o_ref[pl.ds(r0, tq), :] = jnp.concatenate([acc_scr[hd] for hd in range(heads)], axis=0).T
        return 0

    qs0 = load_queries(0)
    for hd in range(heads):
        logits(qs0, 0, 0, hd)
    lax.fori_loop(0, n_query_tiles, q_body, 0)


def _attn_params():
    return pltpu.CompilerParams(dimension_semantics=("parallel", "parallel"), vmem_limit_bytes=VMEM_LIMIT)


def _mla_attn(q, k, vt, tq):
    batch, seq, _ = q.shape
    tk = vt.shape[-1]
    heads = MLA_HEADS_PER_STEP
    qk_spec = pl.BlockSpec((None, seq, heads * LANES), lambda b, g: (b, 0, g))
    return pl.pallas_call(
        functools.partial(_mla_attn_kernel, tq=tq, tk=tk, seq=seq, heads=heads),
        grid=(batch, A_HEADS // heads),
        in_specs=[qk_spec, qk_spec,
                  pl.BlockSpec((None, seq // tk, heads * A_VROWS, tk), lambda b, g: (b, 0, g, 0))],
        out_specs=pl.BlockSpec((None, seq, heads * A_VDIM), lambda b, g: (b, 0, g)),
        out_shape=jax.ShapeDtypeStruct((batch, seq, A_HEADS * A_VDIM), F32),
        scratch_shapes=[pltpu.VMEM((2, heads, tk, tq), F32), pltpu.VMEM((2, heads, 1, tq), F32),
                        pltpu.VMEM((heads, 1, tq), F32), pltpu.VMEM((heads, A_VROWS, tq), F32)],
        compiler_params=_attn_params(),
        name="mla_attn",
    )(q, k, vt)


def _sb_attn(q, k, vt, later2, tq):
    batch, seq, _ = q.shape
    tk = vt.shape[-1]
    heads = SB_HEADS_PER_STEP
    spec = pl.BlockSpec((None, seq, heads * B_DIM), lambda b, g: (b, 0, g))
    return pl.pallas_call(
        functools.partial(_sb_attn_kernel, tq=tq, tk=tk, seq=seq, heads=heads),
        grid=(batch, B_HEADS // heads),
        in_specs=[spec, spec, pl.BlockSpec((None, seq // tk, heads * B_DIM, tk), lambda b, g: (b, 0, g, 0)),
                  pl.BlockSpec(later2.shape, lambda b, g: (0, 0))],
        out_specs=spec,
        out_shape=jax.ShapeDtypeStruct((batch, seq, B_HEADS * B_DIM), F32),
        scratch_shapes=[pltpu.VMEM((2, heads, tk, tq), F32), pltpu.VMEM((heads, 1, tq), F32),
                        pltpu.VMEM((heads, B_DIM, tq), F32)],
        compiler_params=_attn_params(),
        name="sb_attn",
    )(q, k, vt, later2)


def _rope_tables(seq):
    pos = jnp.arange(seq, dtype=F32)
    inv = 1.0 / (ROPE_THETA ** (jnp.arange(0, A_ROPE, 2, dtype=F32) / A_ROPE))
    ang = pos[:, None] * inv[None, :]
    cos, sin = jnp.cos(ang), jnp.sin(ang)
    ones = jnp.ones((seq, A_NOPE), F32)
    zeros_n = jnp.zeros((seq, A_NOPE), F32)
    zeros_p = jnp.zeros((seq, LANES - A_NOPE - A_ROPE), F32)
    cos_t = jnp.concatenate([ones, cos, cos, zeros_p], axis=1)
    sin_t = jnp.concatenate([zeros_n, sin, sin, zeros_p], axis=1)
    return cos_t, sin_t


def _rot_half_cols(w):
    half = A_ROPE // 2
    return jnp.concatenate([-w[..., half:], w[..., :half]], axis=-1)


def _mla_weights(w_in, w_uq, w_ukv):
    d = w_in.shape[0]
    pad = LANES - A_NOPE - A_ROPE
    c_q = w_in[:, :A_QLORA]
    c_kv = w_in[:, A_QLORA:A_QLORA + A_KVLORA]
    k_r = w_in[:, A_QLORA + A_KVLORA:A_QLORA + A_KVLORA + A_ROPE]
    w_gate = w_in[:, A_QLORA + A_KVLORA + A_ROPE:]
    zn = jnp.zeros((d, A_NOPE), F32)
    zp = jnp.zeros((d, pad), F32)
    w_in_p = jnp.concatenate([c_q, c_kv, zn, k_r, zp, zn, _rot_half_cols(k_r), zp], axis=1)

    wq = w_uq.reshape(A_QLORA, A_HEADS, A_NOPE + A_ROPE)
    nope, rope = wq[..., :A_NOPE], wq[..., A_NOPE:]
    zqn = jnp.zeros((A_QLORA, A_HEADS, A_NOPE), F32)
    zqp = jnp.zeros((A_QLORA, A_HEADS, pad), F32)
    q_a = jnp.concatenate([nope, rope, zqp], axis=-1).reshape(A_QLORA, A_HEADS * LANES)
    q_b = jnp.concatenate([zqn, _rot_half_cols(rope), zqp], axis=-1).reshape(A_QLORA, A_HEADS * LANES)
    w_q_p = jnp.concatenate([q_a, q_b], axis=1)

    wkv = w_ukv.reshape(A_KVLORA, A_HEADS, A_NOPE + A_VDIM)
    kn, vv = wkv[..., :A_NOPE], wkv[..., A_NOPE:]
    zk = jnp.zeros((A_KVLORA, A_HEADS, LANES - A_NOPE), F32)
    w_k_p = jnp.concatenate([kn, zk], axis=-1).reshape(A_KVLORA, A_HEADS * LANES)
    w_vt = vv.reshape(A_KVLORA, A_HEADS * A_VDIM).T
    return (w_in_p.astype(BF16), w_gate.astype(BF16), w_q_p.astype(BF16), w_k_p.astype(BF16),
            w_vt.astype(BF16))


def _later_key_matrix(blk):
    s = lax.broadcasted_iota(jnp.int32, (blk, blk), 0)
    j = lax.broadcasted_iota(jnp.int32, (blk, blk), 1)
    return jnp.concatenate([(j > s).astype(BF16), jnp.ones((BF16_SUBLANES, blk), BF16)], axis=0)


def kernel(x, a_norm_pre, a_w_in, a_q_norm, a_w_uq, a_kv_norm, a_w_ukv, a_w_o, a_norm_post,
           b_kv_norm, b_w_kv, b_norm_pre, b_w_in, b_w_o, b_norm_post):
    batch, seq, d = x.shape
    rows = batch * seq
    tq = min(ATTN_TQ, seq)
    tk = min(MLA_TK, seq)
    x2 = x.reshape(rows, d)
    cos_t, sin_t = _rope_tables(seq)
    n_a = a_w_in.shape[0]
    n_b = b_w_in.shape[0]

    for i in range(n_a):
        w_in_p, w_gate, w_q_p, w_k_p, w_vt = _mla_weights(a_w_in[i], a_w_uq[i], a_w_ukv[i])
        q, k, vt = _mla_proj(x2, a_norm_pre[i][None], w_in_p, a_q_norm[i][None], w_q_p,
                             a_kv_norm[i][None], w_k_p, w_vt, cos_t, sin_t, seq, tk)
        o = _mla_attn(q.reshape(batch, seq, -1), k.reshape(batch, seq, -1), vt, tq)
        x2 = _out_proj(x2, o.reshape(rows, -1), a_norm_pre[i][None], w_gate,
                       a_w_o[i].astype(BF16), a_norm_post[i][None])

    width = B_HEADS * B_DIM
    tk = min(SB_TK, seq)
    later2 = _later_key_matrix(min(SB_BLOCK, tk))
    k = vt = None
    for j in range(n_b):
        w_q = b_w_in[j][:, :width].astype(BF16)
        w_gate = b_w_in[j][:, width:].astype(BF16)
        if j == 0:
            q, k, vt = _sb_proj(x2, b_norm_pre[j][None], w_q, seq, tk, b_kv_norm[None],
                                b_w_kv[:, :width].astype(BF16), b_w_kv[:, width:].T.astype(BF16))
            k = k.reshape(batch, seq, width)
        else:
            (q,) = _sb_proj(x2, b_norm_pre[j][None], w_q, seq, tk)
        o = _sb_attn(q.reshape(batch, seq, width), k, vt, later2, tq)
        x2 = _out_proj(x2, o.reshape(rows, width), b_norm_pre[j][None], w_gate,
                       b_w_o[j].astype(BF16), b_norm_post[j][None])
    return x2.reshape(batch, seq, d)
```

```python
import functools
import math

import jax
import jax.numpy as jnp
from jax import lax
from jax.experimental import pallas as pl
from jax.experimental.pallas import tpu as pltpu

D_MODEL = 1024
A_HEADS = 16
A_NOPE = 64
A_ROPE = 32
A_VDIM = 64
A_QLORA = 256
A_KVLORA = 128
B_HEADS = 16
B_DIM = 64
ROPE_THETA = 10000.0
EPS = 1e-6

LANES = 128
ROW_TILE = 512
OUT_ROW_TILE = 1024
ROW_PARTS = 2
BF16_SUBLANES = 16
A_VROWS = A_VDIM + BF16_SUBLANES
ATTN_TQ = 256
MLA_TK = 512
MLA_HEADS_PER_STEP = 4
PREFETCH_HEADS = 2
SB_TK = 256
SB_HEADS_PER_STEP = 8
SB_BLOCK = 128
F32_UNDERFLOW_LOG2 = -152.0
CHUNK = 32
VMEM_LIMIT = 56 * 1024 * 1024
LOG2E = 1.4426950408889634

F32 = jnp.float32
BF16 = jnp.bfloat16


def _rms_scale(x):
    return lax.rsqrt(jnp.mean(x * x, axis=-1, keepdims=True) + EPS)


def _dot(a, b):
    return jnp.dot(a, b, preferred_element_type=F32)


def _dot_nt(a, b):
    return lax.dot_general(a, b, (((1,), (1,)), ((), ())), preferred_element_type=F32)


def _store_key_tiles(vt_ref, v_t, ones_rows=0):
    tile = vt_ref.shape[-1]
    for t in range(vt_ref.shape[0]):
        cols = slice(t * tile, (t + 1) * tile)
        if not ones_rows:
            vt_ref[t] = v_t[:, cols].astype(BF16)
            continue
        per_head = A_VDIM + ones_rows
        for hd in range(v_t.shape[0] // A_VDIM):
            vt_ref[t, hd * per_head:hd * per_head + A_VDIM, :] = v_t[hd * A_VDIM:(hd + 1) * A_VDIM, cols].astype(BF16)
            vt_ref[t, hd * per_head + A_VDIM:(hd + 1) * per_head, :] = jnp.ones((ones_rows, tile), BF16)


def _mla_proj_kernel(x_ref, g_ref, win_ref, qg_ref, wq_ref, kvg_ref, wk_ref, wvt_ref, cos_ref, sin_ref,
                     q_ref, k_ref, vt_ref):
    x = x_ref[...]
    h = x * _rms_scale(x) * g_ref[...]
    proj = _dot(h.astype(BF16), win_ref[...])
    cq = proj[:, :A_QLORA]
    ckv = proj[:, A_QLORA:A_QLORA + A_KVLORA]
    kr_blk = proj[:, A_QLORA + A_KVLORA:]
    cqn = cq * _rms_scale(cq) * qg_ref[...]
    q2 = _dot(cqn.astype(BF16), wq_ref[...])
    ckvn = (ckv * _rms_scale(ckv) * kvg_ref[...]).astype(BF16)
    kn = _dot(ckvn, wk_ref[...])
    _store_key_tiles(vt_ref, _dot_nt(wvt_ref[...], ckvn), BF16_SUBLANES)
    cos = cos_ref[...]
    sin = sin_ref[...]
    kr = kr_blk * cos + pltpu.roll(kr_blk, LANES - A_ROPE, 1) * sin
    scale = LOG2E / math.sqrt(A_NOPE + A_ROPE)
    cos_q = cos * scale
    sin_q = sin * scale
    for hd in range(A_HEADS):
        sl = slice(hd * LANES, (hd + 1) * LANES)
        q_blk = q2[:, sl]
        q_ref[:, sl] = (q_blk * cos_q + pltpu.roll(q_blk, LANES - A_ROPE, 1) * sin_q).astype(BF16)
        k_ref[:, sl] = (kn[:, sl] + kr).astype(BF16)


def _sb_proj_kernel(*refs, with_kv):
    if with_kv:
        x_ref, gq_ref, wq_ref, gkv_ref, wk_ref, wvt_ref, q_ref, k_ref, vt_ref = refs
    else:
        x_ref, gq_ref, wq_ref, q_ref = refs
    x = x_ref[...]
    y = x * _rms_scale(x)
    q = _dot((y * gq_ref[...]).astype(BF16), wq_ref[...])
    q_ref[...] = (q * (LOG2E / math.sqrt(B_DIM))).astype(BF16)
    if with_kv:
        h_kv = (y * gkv_ref[...]).astype(BF16)
        k_ref[...] = _dot(h_kv, wk_ref[...]).astype(BF16)
        _store_key_tiles(vt_ref, _dot_nt(wvt_ref[...], h_kv))


def _row_parts(tm):
    part = tm // ROW_PARTS
    return [slice(i * part, (i + 1) * part) for i in range(ROW_PARTS)]


def _out_kernel(x_ref, o_ref, gpre_ref, wg_ref, wo_ref, gpost_ref, xn_ref):
    parts = _row_parts(x_ref.shape[0])
    gates = []
    for rows in parts:
        x = x_ref[rows, :]
        h = x * _rms_scale(x) * gpre_ref[...]
        gates.append(_dot(h.astype(BF16), wg_ref[...]))
    outs = []
    for rows, gate in zip(parts, gates):
        og = o_ref[rows, :] * (gate * (1.0 / (1.0 + jnp.exp(-gate))))
        outs.append(_dot(og.astype(BF16), wo_ref[...]))
    for rows, out in zip(parts, outs):
        xn_ref[rows, :] = x_ref[rows, :] + out * _rms_scale(out) * gpost_ref[...]


def _row_spec(tm, width):
    return pl.BlockSpec((tm, width), lambda i: (i, 0))


def _full_spec(shape):
    return pl.BlockSpec(shape, lambda i: (0,) * len(shape))


def _row_params():
    return pltpu.CompilerParams(dimension_semantics=("parallel",), vmem_limit_bytes=VMEM_LIMIT)


def _vt_out(rows, seq, tm, tile, width):
    tiles_per_seq = seq // tm
    spec = pl.BlockSpec((None, tm // tile, width, tile), lambda i: (i // tiles_per_seq, i % tiles_per_seq, 0, 0))
    return spec, jax.ShapeDtypeStruct((rows // seq, seq // tile, width, tile), BF16)


def _mla_proj(x2, g, w_in_p, qg, w_q_p, kvg, w_k_p, w_vt, cos_t, sin_t, seq, tile):
    rows = x2.shape[0]
    tm = min(ROW_TILE, seq)
    tiles_per_seq = seq // tm
    hw = A_HEADS * LANES
    table_spec = pl.BlockSpec((tm, LANES), lambda i: (i % tiles_per_seq, 0))
    vt_spec, vt_shape = _vt_out(rows, seq, tm, tile, A_HEADS * A_VROWS)
    return pl.pallas_call(
        _mla_proj_kernel,
        grid=(rows // tm,),
        in_specs=[_row_spec(tm, D_MODEL), _full_spec(g.shape), _full_spec(w_in_p.shape),
                  _full_spec(qg.shape), _full_spec(w_q_p.shape), _full_spec(kvg.shape),
                  _full_spec(w_k_p.shape), _full_spec(w_vt.shape), table_spec, table_spec],
        out_specs=[_row_spec(tm, hw), _row_spec(tm, hw), vt_spec],
        out_shape=[jax.ShapeDtypeStruct((rows, hw), BF16), jax.ShapeDtypeStruct((rows, hw), BF16), vt_shape],
        compiler_params=_row_params(),
        name="mla_proj",
    )(x2, g, w_in_p, qg, w_q_p, kvg, w_k_p, w_vt, cos_t, sin_t)


def _sb_proj(x2, gq, w_q, seq, tile, gkv=None, w_k=None, w_vt=None):
    rows = x2.shape[0]
    tm = min(ROW_TILE, seq)
    width = B_HEADS * B_DIM
    with_kv = w_k is not None
    args = [x2, gq, w_q] + ([gkv, w_k, w_vt] if with_kv else [])
    in_specs = [_row_spec(tm, D_MODEL)] + [_full_spec(a.shape) for a in args[1:]]
    out_specs = [_row_spec(tm, width)]
    out_shape = [jax.ShapeDtypeStruct((rows, width), BF16)]
    if with_kv:
        vt_spec, vt_shape = _vt_out(rows, seq, tm, tile, width)
        out_specs += [_row_spec(tm, width), vt_spec]
        out_shape += [jax.ShapeDtypeStruct((rows, width), BF16), vt_shape]
    return pl.pallas_call(
        functools.partial(_sb_proj_kernel, with_kv=with_kv),
        grid=(rows // tm,),
        in_specs=in_specs,
        out_specs=out_specs,
        out_shape=out_shape,
        compiler_params=_row_params(),
        name="sb_proj_kv" if with_kv else "sb_proj",
    )(*args)


def _out_proj(x2, o2, gpre, w_gate, w_o, gpost):
    rows = x2.shape[0]
    tm = min(OUT_ROW_TILE, rows)
    return pl.pallas_call(
        _out_kernel,
        grid=(rows // tm,),
        in_specs=[_row_spec(tm, D_MODEL), _row_spec(tm, o2.shape[1]), _full_spec(gpre.shape),
                  _full_spec(w_gate.shape), _full_spec(w_o.shape), _full_spec(gpost.shape)],
        out_specs=_row_spec(tm, D_MODEL),
        out_shape=jax.ShapeDtypeStruct((rows, D_MODEL), F32),
        compiler_params=_row_params(),
        name="out_proj",
    )(x2, o2, gpre, w_gate, w_o, gpost)


def _mla_attn_kernel(q_ref, k_ref, vt_ref, o_ref, s_scr, max_scr, m_scr, acc_scr, *, tq, tk, seq, heads):
    n_query_tiles = seq // tq
    key = lax.broadcasted_iota(jnp.int32, (CHUNK, tq), 0)
    qry = lax.broadcasted_iota(jnp.int32, (CHUNK, tq), 1)

    def load_queries(qi):
        r0 = pl.multiple_of(qi * tq, tq)
        return [q_ref[pl.ds(r0, tq), hd * LANES:(hd + 1) * LANES] for hd in range(heads)]

    def score_head(qs, j, slot, hd):
        c0 = pl.multiple_of(j * tk, tk)
        s = _dot_nt(k_ref[pl.ds(c0, tk), hd * LANES:(hd + 1) * LANES], qs[hd])
        s_scr[slot, hd] = s
        max_scr[slot, hd] = jnp.max(s, axis=0, keepdims=True)

    def score_chunk(slot, hd, c, diag_offset):
        blk = s_scr[slot, hd, c * CHUNK:(c + 1) * CHUNK, :]
        if diag_offset is None or (c + 1) * CHUNK - 1 <= diag_offset:
            return blk
        return jnp.where(key + (c * CHUNK - diag_offset) <= qry, blk, -jnp.inf)

    def tile(j, slot, prefetch, diag_offset=None):
        n_chunks = tk // CHUNK if diag_offset is None else (diag_offset + tq) // CHUNK
        pending = None
        for hd in range(heads):
            for ahead in range(PREFETCH_HEADS * hd, min(PREFETCH_HEADS * (hd + 1), heads)):
                prefetch(ahead)
            if diag_offset is None:
                mx = max_scr[slot, hd]
            else:
                mx = score_chunk(slot, hd, 0, diag_offset)
                for c in range(1, n_chunks):
                    mx = jnp.maximum(mx, score_chunk(slot, hd, c, diag_offset))
                mx = jnp.max(mx, axis=0, keepdims=True)
            m = m_scr[hd]
            m_new = jnp.maximum(m, mx)
            m_scr[hd] = m_new
            alpha = jnp.exp2(m - m_new)
            ps = [jnp.exp2(score_chunk(slot, hd, c, diag_offset) - m_new).astype(BF16)
                  for c in range(n_chunks)]
            pv = _dot(vt_ref[j, hd * A_VROWS:(hd + 1) * A_VROWS, :n_chunks * CHUNK], jnp.concatenate(ps, axis=0))
            if pending is not None:
                acc_scr[pending[0]] = pending[1] * acc_scr[pending[0]] + pending[2]
            pending = (hd, alpha, pv)
        acc_scr[pending[0]] = pending[1] * acc_scr[pending[0]] + pending[2]

    def q_body(qi, first_buffer):
        r0 = pl.multiple_of(qi * tq, tq)
        qs = load_queries(qi)
        diag = (qi * tq) // tk
        for hd in range(heads):
            m_scr[hd] = jnp.full((1, tq), -jnp.inf, F32)
            acc_scr[hd] = jnp.zeros((A_VROWS, tq), F32)

        def body(j, _):
            for slot in range(2):
                @pl.when((j + first_buffer) % 2 == slot)
                def _():
                    tile(j, slot, lambda hd: score_head(qs, j + 1, 1 - slot, hd))
            return 0

        lax.fori_loop(0, diag, body, 0)
        diag_buffer = (diag + first_buffer) % 2
        position = (r0 - diag * tk) // tq
        for slot in range(2):
            for pos in range(tk // tq):
                @pl.when(jnp.logical_and(diag_buffer == slot, position == pos))
                def _():
                    qs_next = load_queries(jnp.minimum(qi + 1, n_query_tiles - 1))
                    tile(diag, slot, lambda hd: score_head(qs_next, 0, 1 - slot, hd), pos * tq)
                    outs = [acc_scr[hd, :A_VDIM, :] / acc_scr[hd, A_VDIM:A_VDIM + 1, :] for hd in range(heads)]
                    o_ref[pl.ds(r0, tq), :] = jnp.concatenate(outs, axis=0).T
        return 1 - diag_buffer

    qs0 = load_queries(0)
    for hd in range(heads):
        score_head(qs0, 0, 0, hd)
    lax.fori_loop(0, n_query_tiles, q_body, jnp.int32(0))


def _sb_attn_kernel(q_ref, k_ref, vt_ref, later_ref, o_ref, z_scr, later_scr, acc_scr, *, tq, tk, seq, heads):
    blk = later_ref.shape[1]
    n_query_tiles = seq // tq
    lane = lax.broadcasted_iota(jnp.int32, (tq, LANES), 1)
    key = lax.broadcasted_iota(jnp.int32, (CHUNK, tq), 0)
    qry = lax.broadcasted_iota(jnp.int32, (CHUNK, tq), 1)

    def load_queries(qi):
        r0 = pl.multiple_of(qi * tq, tq)
        qs = []
        for hd in range(heads):
            pair = q_ref[pl.ds(r0, tq), (hd // 2) * LANES:(hd // 2 + 1) * LANES]
            mine = (lane < B_DIM) if hd % 2 == 0 else (lane >= B_DIM)
            qs.append(jnp.where(mine, pair, jnp.zeros_like(pair)))
        return qs

    def logits(qs, j, slot, hd):
        c0 = pl.multiple_of(j * tk, tk)
        k = k_ref[pl.ds(c0, tk), (hd // 2) * LANES:(hd // 2 + 1) * LANES]
        z_scr[slot, hd] = _dot_nt(k, qs[hd])

    def q_body(qi, _):
        r0 = pl.multiple_of(qi * tq, tq)
        qs = load_queries(qi)
        diag = (qi * tq) // tk

        def cumulative_sums(slot, hd, b, strict_limit):
            parts = []
            for c in range(blk // CHUNK):
                rows = slice(b * blk + c * CHUNK, b * blk + (c + 1) * CHUNK)
                z = z_scr[slot, hd, rows, :]
                neg_lse = jnp.log(1.0 + jnp.exp2(-jnp.abs(z))) * (-LOG2E)
                log_rest = neg_lse - jnp.maximum(z, 0.0)
                z_scr[slot, hd, rows, :] = log_rest + z
                if strict_limit is not None:
                    log_rest = jnp.where(key + (b * blk + c * CHUNK) < strict_limit, log_rest, 0.0)
                parts.append(log_rest.astype(BF16))
            return _dot(later_ref[...], jnp.concatenate(parts, axis=0))

        def weigh_values(j, slot, hd, b, sums, later, strict_limit):
            weights = []
            for c in range(blk // CHUNK):
                rows = slice(b * blk + c * CHUNK, b * blk + (c + 1) * CHUNK)
                a = jnp.exp2(z_scr[slot, hd, rows, :] + sums[c * CHUNK:(c + 1) * CHUNK, :] + later)
                if strict_limit is not None:
                    a = jnp.where(key + (b * blk + c * CHUNK) < strict_limit, a, 0.0)
                weights.append(a.astype(BF16))
            vt = vt_ref[j, hd * B_DIM:(hd + 1) * B_DIM, b * blk:(b + 1) * blk]
            return later + sums[blk:blk + 1], _dot(vt, jnp.concatenate(weights, axis=0))

        def tile(j, slot, strict_limit=None, ahead=None, recycle=None):
            blocks = list(reversed(range(tk // blk)))
            pending = None
            for hd in range(heads + 1):
                sums = [cumulative_sums(slot, hd, b, strict_limit) for b in blocks] if hd < heads else None
                if ahead is not None and hd < heads:
                    logits(qs, ahead[0], ahead[1], hd)
                if pending is not None:
                    later = later_scr[hd - 1]
                    for b, block_sums in zip(blocks, pending):
                        later, weighted = weigh_values(j, slot, hd - 1, b, block_sums, later, strict_limit)
                        acc_scr[hd - 1] += weighted
                    later_scr[hd - 1] = later
                    if recycle is not None:
                        logits(recycle[0], recycle[1], slot, hd - 1)
                pending = sums

        def any_weight_left():
            worst = later_scr[0]
            for hd in range(1, heads):
                worst = jnp.maximum(worst, later_scr[hd])
            return (jnp.max(worst) > F32_UNDERFLOW_LOG2).astype(jnp.int32)

        def more(state):
            n, live = state
            return jnp.logical_and(n <= diag, live > 0)

        def body(state):
            n, _ = state
            j = diag - n

            @pl.when(n >= 2)
            def _():
                for hd in range(heads):
                    logits(qs, j, 1, hd)

            tile(j, 1)
            return n + 1, any_weight_left()

        for hd in range(heads):
            later_scr[hd] = jnp.zeros((1, tq), F32)
            acc_scr[hd] = jnp.zeros((B_DIM, tq), F32)
        qi_next = jnp.minimum(qi + 1, n_query_tiles - 1)
        tile(diag, 0, qry + (r0 - diag * tk), ahead=(jnp.maximum(diag - 1, 0), 1),
             recycle=(load_queries(qi_next), (qi_next * tq) // tk))
        lax.while_loop(more, body, (jnp.int32(1), any_weight_left()))
        o_ref[pl.ds(r0, tq), :] = jnp.concatenate([acc_scr[hd] for hd in range(heads)], axis=0).T
        return 0

    qs0 = load_queries(0)
    for hd in range(heads):
        logits(qs0, 0, 0, hd)
    lax.fori_loop(0, n_query_tiles, q_body, 0)


def _attn_params():
    return pltpu.CompilerParams(dimension_semantics=("parallel", "parallel"), vmem_limit_bytes=VMEM_LIMIT)


def _mla_attn(q, k, vt, tq):
    batch, seq, _ = q.shape
    tk = vt.shape[-1]
    heads = MLA_HEADS_PER_STEP
    qk_spec = pl.BlockSpec((None, seq, heads * LANES), lambda b, g: (b, 0, g))
    return pl.pallas_call(
        functools.partial(_mla_attn_kernel, tq=tq, tk=tk, seq=seq, heads=heads),
        grid=(batch, A_HEADS // heads),
        in_specs=[qk_spec, qk_spec,
                  pl.BlockSpec((None, seq // tk, heads * A_VROWS, tk), lambda b, g: (b, 0, g, 0))],
        out_specs=pl.BlockSpec((None, seq, heads * A_VDIM), lambda b, g: (b, 0, g)),
        out_shape=jax.ShapeDtypeStruct((batch, seq, A_HEADS * A_VDIM), F32),
        scratch_shapes=[pltpu.VMEM((2, heads, tk, tq), F32), pltpu.VMEM((2, heads, 1, tq), F32),
                        pltpu.VMEM((heads, 1, tq), F32), pltpu.VMEM((heads, A_VROWS, tq), F32)],
        compiler_params=_attn_params(),
        name="mla_attn",
    )(q, k, vt)


def _sb_attn(q, k, vt, later2, tq):
    batch, seq, _ = q.shape
    tk = vt.shape[-1]
    heads = SB_HEADS_PER_STEP
    spec = pl.BlockSpec((None, seq, heads * B_DIM), lambda b, g: (b, 0, g))
    return pl.pallas_call(
        functools.partial(_sb_attn_kernel, tq=tq, tk=tk, seq=seq, heads=heads),
        grid=(batch, B_HEADS // heads),
        in_specs=[spec, spec, pl.BlockSpec((None, seq // tk, heads * B_DIM, tk), lambda b, g: (b, 0, g, 0)),
                  pl.BlockSpec(later2.shape, lambda b, g: (0, 0))],
        out_specs=spec,
        out_shape=jax.ShapeDtypeStruct((batch, seq, B_HEADS * B_DIM), F32),
        scratch_shapes=[pltpu.VMEM((2, heads, tk, tq), F32), pltpu.VMEM((heads, 1, tq), F32),
                        pltpu.VMEM((heads, B_DIM, tq), F32)],
        compiler_params=_attn_params(),
        name="sb_attn",
    )(q, k, vt, later2)


def _rope_tables(seq):
    pos = jnp.arange(seq, dtype=F32)
    inv = 1.0 / (ROPE_THETA ** (jnp.arange(0, A_ROPE, 2, dtype=F32) / A_ROPE))
    ang = pos[:, None] * inv[None, :]
    cos, sin = jnp.cos(ang), jnp.sin(ang)
    ones = jnp.ones((seq, A_NOPE), F32)
    zeros_n = jnp.zeros((seq, A_NOPE), F32)
    zeros_p = jnp.zeros((seq, LANES - A_NOPE - A_ROPE), F32)
    cos_t = jnp.concatenate([ones, cos, cos, zeros_p], axis=1)
    sin_t = jnp.concatenate([zeros_n, sin, sin, zeros_p], axis=1)
    return cos_t, sin_t


def _rot_half_cols(w):
    half = A_ROPE // 2
    return jnp.concatenate([-w[..., half:], w[..., :half]], axis=-1)


def _mla_weights(w_in, w_uq, w_ukv):
    d = w_in.shape[0]
    pad = LANES - A_NOPE - A_ROPE
    c_q = w_in[:, :A_QLORA]
    c_kv = w_in[:, A_QLORA:A_QLORA + A_KVLORA]
    k_r = w_in[:, A_QLORA + A_KVLORA:A_QLORA + A_KVLORA + A_ROPE]
    w_gate = w_in[:, A_QLORA + A_KVLORA + A_ROPE:]
    assert pad == A_ROPE
    zn = jnp.zeros((d, A_NOPE), F32)
    w_in_p = jnp.concatenate([c_q, c_kv, zn, k_r, _rot_half_cols(k_r)], axis=1)

    wq = w_uq.reshape(A_QLORA, A_HEADS, A_NOPE + A_ROPE)
    nope, rope = wq[..., :A_NOPE], wq[..., A_NOPE:]
    w_q_p = jnp.concatenate([nope, rope, _rot_half_cols(rope)], axis=-1).reshape(A_QLORA, A_HEADS * LANES)

    wkv = w_ukv.reshape(A_KVLORA, A_HEADS, A_NOPE + A_VDIM)
    kn, vv = wkv[..., :A_NOPE], wkv[..., A_NOPE:]
    zk = jnp.zeros((A_KVLORA, A_HEADS, LANES - A_NOPE), F32)
    w_k_p = jnp.concatenate([kn, zk], axis=-1).reshape(A_KVLORA, A_HEADS * LANES)
    w_vt = vv.reshape(A_KVLORA, A_HEADS * A_VDIM).T
    return (w_in_p.astype(BF16), w_gate.astype(BF16), w_q_p.astype(BF16), w_k_p.astype(BF16),
            w_vt.astype(BF16))


def _later_key_matrix(blk):
    s = lax.broadcasted_iota(jnp.int32, (blk, blk), 0)
    j = lax.broadcasted_iota(jnp.int32, (blk, blk), 1)
    return jnp.concatenate([(j > s).astype(BF16), jnp.ones((BF16_SUBLANES, blk), BF16)], axis=0)


def kernel(x, a_norm_pre, a_w_in, a_q_norm, a_w_uq, a_kv_norm, a_w_ukv, a_w_o, a_norm_post,
           b_kv_norm, b_w_kv, b_norm_pre, b_w_in, b_w_o, b_norm_post):
    batch, seq, d = x.shape
    rows = batch * seq
    tq = min(ATTN_TQ, seq)
    tk = min(MLA_TK, seq)
    x2 = x.reshape(rows, d)
    cos_t, sin_t = _rope_tables(seq)
    n_a = a_w_in.shape[0]
    n_b = b_w_in.shape[0]

    for i in range(n_a):
        w_in_p, w_gate, w_q_p, w_k_p, w_vt = _mla_weights(a_w_in[i], a_w_uq[i], a_w_ukv[i])
        q, k, vt = _mla_proj(x2, a_norm_pre[i][None], w_in_p, a_q_norm[i][None], w_q_p,
                             a_kv_norm[i][None], w_k_p, w_vt, cos_t, sin_t, seq, tk)
        o = _mla_attn(q.reshape(batch, seq, -1), k.reshape(batch, seq, -1), vt, tq)
        x2 = _out_proj(x2, o.reshape(rows, -1), a_norm_pre[i][None], w_gate,
                       a_w_o[i].astype(BF16), a_norm_post[i][None])

    width = B_HEADS * B_DIM
    tk = min(SB_TK, seq)
    later2 = _later_key_matrix(min(SB_BLOCK, tk))
    k = vt = None
    for j in range(n_b):
        w_q = b_w_in[j][:, :width].astype(BF16)
        w_gate = b_w_in[j][:, width:].astype(BF16)
        if j == 0:
            q, k, vt = _sb_proj(x2, b_norm_pre[j][None], w_q, seq, tk, b_kv_norm[None],
                                b_w_kv[:, :width].astype(BF16), b_w_kv[:, width:].T.astype(BF16))
            k = k.reshape(batch, seq, width)
        else:
            (q,) = _sb_proj(x2, b_norm_pre[j][None], w_q, seq, tk)
        o = _sb_attn(q.reshape(batch, seq, width), k, vt, later2, tq)
        x2 = _out_proj(x2, o.reshape(rows, width), b_norm_pre[j][None], w_gate,
                       b_w_o[j].astype(BF16), b_norm_post[j][None])
    return x2.reshape(batch, seq, d)
```

```python
import functools
import math

import jax
import jax.numpy as jnp
from jax import lax
from jax.experimental import pallas as pl
from jax.experimental.pallas import tpu as pltpu

D_MODEL = 1024
A_HEADS = 16
A_NOPE = 64
A_ROPE = 32
A_VDIM = 64
A_QLORA = 256
A_KVLORA = 128
B_HEADS = 16
B_DIM = 64
ROPE_THETA = 10000.0
EPS = 1e-6

LANES = 128
ROW_TILE = 512
OUT_ROW_TILE = 1024
ROW_PARTS = 2
BF16_SUBLANES = 16
A_VROWS = A_VDIM + BF16_SUBLANES
ATTN_TQ = 256
MLA_TK = 512
MLA_HEADS_PER_STEP = 4
PREFETCH_HEADS = 2
SB_TK = 256
SB_HEADS_PER_STEP = 8
SB_BLOCK = 128
F32_UNDERFLOW_LOG2 = float("-inf")
CHUNK = 32
VMEM_LIMIT = 56 * 1024 * 1024
LOG2E = 1.4426950408889634

F32 = jnp.float32
BF16 = jnp.bfloat16


def _rms_scale(x):
    return lax.rsqrt(jnp.mean(x * x, axis=-1, keepdims=True) + EPS)


def _dot(a, b):
    return jnp.dot(a, b, preferred_element_type=F32)


def _dot_nt(a, b):
    return lax.dot_general(a, b, (((1,), (1,)), ((), ())), preferred_element_type=F32)


def _store_key_tiles(vt_ref, v_t, ones_rows=0):
    tile = vt_ref.shape[-1]
    for t in range(vt_ref.shape[0]):
        cols = slice(t * tile, (t + 1) * tile)
        if not ones_rows:
            vt_ref[t] = v_t[:, cols].astype(BF16)
            continue
        per_head = A_VDIM + ones_rows
        for hd in range(v_t.shape[0] // A_VDIM):
            vt_ref[t, hd * per_head:hd * per_head + A_VDIM, :] = v_t[hd * A_VDIM:(hd + 1) * A_VDIM, cols].astype(BF16)
            vt_ref[t, hd * per_head + A_VDIM:(hd + 1) * per_head, :] = jnp.ones((ones_rows, tile), BF16)


def _mla_proj_kernel(x_ref, g_ref, win_ref, qg_ref, wq_ref, kvg_ref, wk_ref, wvt_ref, cos_ref, sin_ref,
                     q_ref, k_ref, vt_ref):
    x = x_ref[...]
    h = x * _rms_scale(x) * g_ref[...]
    proj = _dot(h.astype(BF16), win_ref[...])
    cq = proj[:, :A_QLORA]
    ckv = proj[:, A_QLORA:A_QLORA + A_KVLORA]
    kr_blk = proj[:, A_QLORA + A_KVLORA:]
    cqn = cq * _rms_scale(cq) * qg_ref[...]
    q2 = _dot(cqn.astype(BF16), wq_ref[...])
    ckvn = (ckv * _rms_scale(ckv) * kvg_ref[...]).astype(BF16)
    kn = _dot(ckvn, wk_ref[...])
    _store_key_tiles(vt_ref, _dot_nt(wvt_ref[...], ckvn), BF16_SUBLANES)
    cos = cos_ref[...]
    sin = sin_ref[...]
    kr = kr_blk * cos + pltpu.roll(kr_blk, LANES - A_ROPE, 1) * sin
    scale = LOG2E / math.sqrt(A_NOPE + A_ROPE)
    cos_q = cos * scale
    sin_q = sin * scale
    for hd in range(A_HEADS):
        sl = slice(hd * LANES, (hd + 1) * LANES)
        q_blk = q2[:, sl]
        q_ref[:, sl] = (q_blk * cos_q + pltpu.roll(q_blk, LANES - A_ROPE, 1) * sin_q).astype(BF16)
        k_ref[:, sl] = (kn[:, sl] + kr).astype(BF16)


def _sb_proj_kernel(*refs, with_kv):
    if with_kv:
        x_ref, gq_ref, wq_ref, gkv_ref, wk_ref, wvt_ref, q_ref, k_ref, vt_ref = refs
    else:
        x_ref, gq_ref, wq_ref, q_ref = refs
    x = x_ref[...]
    y = x * _rms_scale(x)
    q = _dot((y * gq_ref[...]).astype(BF16), wq_ref[...])
    q_ref[...] = (q * (LOG2E / math.sqrt(B_DIM))).astype(BF16)
    if with_kv:
        h_kv = (y * gkv_ref[...]).astype(BF16)
        k_ref[...] = _dot(h_kv, wk_ref[...]).astype(BF16)
        _store_key_tiles(vt_ref, _dot_nt(wvt_ref[...], h_kv))


def _row_parts(tm):
    part = tm // ROW_PARTS
    return [slice(i * part, (i + 1) * part) for i in range(ROW_PARTS)]


def _out_kernel(x_ref, o_ref, gpre_ref, wg_ref, wo_ref, gpost_ref, xn_ref):
    parts = _row_parts(x_ref.shape[0])
    gates = []
    for rows in parts:
        x = x_ref[rows, :]
        h = x * _rms_scale(x) * gpre_ref[...]
        gates.append(_dot(h.astype(BF16), wg_ref[...]))
    outs = []
    for rows, gate in zip(parts, gates):
        og = o_ref[rows, :] * (gate * (1.0 / (1.0 + jnp.exp(-gate))))
        outs.append(_dot(og.astype(BF16), wo_ref[...]))
    for rows, out in zip(parts, outs):
        xn_ref[rows, :] = x_ref[rows, :] + out * _rms_scale(out) * gpost_ref[...]


def _row_spec(tm, width):
    return pl.BlockSpec((tm, width), lambda i: (i, 0))


def _full_spec(shape):
    return pl.BlockSpec(shape, lambda i: (0,) * len(shape))


def _row_params():
    return pltpu.CompilerParams(dimension_semantics=("parallel",), vmem_limit_bytes=VMEM_LIMIT)


def _vt_out(rows, seq, tm, tile, width):
    tiles_per_seq = seq // tm
    spec = pl.BlockSpec((None, tm // tile, width, tile), lambda i: (i // tiles_per_seq, i % tiles_per_seq, 0, 0))
    return spec, jax.ShapeDtypeStruct((rows // seq, seq // tile, width, tile), BF16)


def _mla_proj(x2, g, w_in_p, qg, w_q_p, kvg, w_k_p, w_vt, cos_t, sin_t, seq, tile):
    rows = x2.shape[0]
    tm = min(ROW_TILE, seq)
    tiles_per_seq = seq // tm
    hw = A_HEADS * LANES
    table_spec = pl.BlockSpec((tm, LANES), lambda i: (i % tiles_per_seq, 0))
    vt_spec, vt_shape = _vt_out(rows, seq, tm, tile, A_HEADS * A_VROWS)
    return pl.pallas_call(
        _mla_proj_kernel,
        grid=(rows // tm,),
        in_specs=[_row_spec(tm, D_MODEL), _full_spec(g.shape), _full_spec(w_in_p.shape),
                  _full_spec(qg.shape), _full_spec(w_q_p.shape), _full_spec(kvg.shape),
                  _full_spec(w_k_p.shape), _full_spec(w_vt.shape), table_spec, table_spec],
        out_specs=[_row_spec(tm, hw), _row_spec(tm, hw), vt_spec],
        out_shape=[jax.ShapeDtypeStruct((rows, hw), BF16), jax.ShapeDtypeStruct((rows, hw), BF16), vt_shape],
        compiler_params=_row_params(),
        name="mla_proj",
    )(x2, g, w_in_p, qg, w_q_p, kvg, w_k_p, w_vt, cos_t, sin_t)


def _sb_proj(x2, gq, w_q, seq, tile, gkv=None, w_k=None, w_vt=None):
    rows = x2.shape[0]
    tm = min(ROW_TILE, seq)
    width = B_HEADS * B_DIM
    with_kv = w_k is not None
    args = [x2, gq, w_q] + ([gkv, w_k, w_vt] if with_kv else [])
    in_specs = [_row_spec(tm, D_MODEL)] + [_full_spec(a.shape) for a in args[1:]]
    out_specs = [_row_spec(tm, width)]
    out_shape = [jax.ShapeDtypeStruct((rows, width), BF16)]
    if with_kv:
        vt_spec, vt_shape = _vt_out(rows, seq, tm, tile, width)
        out_specs += [_row_spec(tm, width), vt_spec]
        out_shape += [jax.ShapeDtypeStruct((rows, width), BF16), vt_shape]
    return pl.pallas_call(
        functools.partial(_sb_proj_kernel, with_kv=with_kv),
        grid=(rows // tm,),
        in_specs=in_specs,
        out_specs=out_specs,
        out_shape=out_shape,
        compiler_params=_row_params(),
        name="sb_proj_kv" if with_kv else "sb_proj",
    )(*args)


def _out_proj(x2, o2, gpre, w_gate, w_o, gpost):
    rows = x2.shape[0]
    tm = min(OUT_ROW_TILE, rows)
    return pl.pallas_call(
        _out_kernel,
        grid=(rows // tm,),
        in_specs=[_row_spec(tm, D_MODEL), _row_spec(tm, o2.shape[1]), _full_spec(gpre.shape),
                  _full_spec(w_gate.shape), _full_spec(w_o.shape), _full_spec(gpost.shape)],
        out_specs=_row_spec(tm, D_MODEL),
        out_shape=jax.ShapeDtypeStruct((rows, D_MODEL), F32),
        compiler_params=_row_params(),
        name="out_proj",
    )(x2, o2, gpre, w_gate, w_o, gpost)


def _mla_attn_kernel(q_ref, k_ref, vt_ref, o_ref, s_scr, max_scr, m_scr, acc_scr, *, tq, tk, seq, heads):
    n_query_tiles = seq // tq
    key = lax.broadcasted_iota(jnp.int32, (CHUNK, tq), 0)
    qry = lax.broadcasted_iota(jnp.int32, (CHUNK, tq), 1)

    def load_queries(qi):
        r0 = pl.multiple_of(qi * tq, tq)
        return [q_ref[pl.ds(r0, tq), hd * LANES:(hd + 1) * LANES] for hd in range(heads)]

    def score_head(qs, j, slot, hd):
        c0 = pl.multiple_of(j * tk, tk)
        s = _dot_nt(k_ref[pl.ds(c0, tk), hd * LANES:(hd + 1) * LANES], qs[hd])
        s_scr[slot, hd] = s
        max_scr[slot, hd] = jnp.max(s, axis=0, keepdims=True)

    def score_chunk(slot, hd, c, diag_offset):
        blk = s_scr[slot, hd, c * CHUNK:(c + 1) * CHUNK, :]
        if diag_offset is None or (c + 1) * CHUNK - 1 <= diag_offset:
            return blk
        return jnp.where(key + (c * CHUNK - diag_offset) <= qry, blk, -jnp.inf)

    def tile(j, slot, prefetch, diag_offset=None):
        n_chunks = tk // CHUNK if diag_offset is None else (diag_offset + tq) // CHUNK
        pending = None
        for hd in range(heads):
            for ahead in range(PREFETCH_HEADS * hd, min(PREFETCH_HEADS * (hd + 1), heads)):
                prefetch(ahead)
            if diag_offset is None:
                mx = max_scr[slot, hd]
            else:
                mx = score_chunk(slot, hd, 0, diag_offset)
                for c in range(1, n_chunks):
                    mx = jnp.maximum(mx, score_chunk(slot, hd, c, diag_offset))
                mx = jnp.max(mx, axis=0, keepdims=True)
            m = m_scr[hd]
            m_new = jnp.maximum(m, mx)
            m_scr[hd] = m_new
            alpha = jnp.exp2(m - m_new)
            ps = [jnp.exp2(score_chunk(slot, hd, c, diag_offset) - m_new).astype(BF16)
                  for c in range(n_chunks)]
            pv = _dot(vt_ref[j, hd * A_VROWS:(hd + 1) * A_VROWS, :n_chunks * CHUNK], jnp.concatenate(ps, axis=0))
            if pending is not None:
                acc_scr[pending[0]] = pending[1] * acc_scr[pending[0]] + pending[2]
            pending = (hd, alpha, pv)
        acc_scr[pending[0]] = pending[1] * acc_scr[pending[0]] + pending[2]

    def q_body(qi, first_buffer):
        r0 = pl.multiple_of(qi * tq, tq)
        qs = load_queries(qi)
        diag = (qi * tq) // tk
        for hd in range(heads):
            m_scr[hd] = jnp.full((1, tq), -jnp.inf, F32)
            acc_scr[hd] = jnp.zeros((A_VROWS, tq), F32)

        def body(j, _):
            for slot in range(2):
                @pl.when((j + first_buffer) % 2 == slot)
                def _():
                    tile(j, slot, lambda hd: score_head(qs, j + 1, 1 - slot, hd))
            return 0

        lax.fori_loop(0, diag, body, 0)
        diag_buffer = (diag + first_buffer) % 2
        position = (r0 - diag * tk) // tq
        for slot in range(2):
            for pos in range(tk // tq):
                @pl.when(jnp.logical_and(diag_buffer == slot, position == pos))
                def _():
                    qs_next = load_queries(jnp.minimum(qi + 1, n_query_tiles - 1))
                    tile(diag, slot, lambda hd: score_head(qs_next, 0, 1 - slot, hd), pos * tq)
                    outs = [acc_scr[hd, :A_VDIM, :] / acc_scr[hd, A_VDIM:A_VDIM + 1, :] for hd in range(heads)]
                    o_ref[pl.ds(r0, tq), :] = jnp.concatenate(outs, axis=0).T
        return 1 - diag_buffer

    qs0 = load_queries(0)
    for hd in range(heads):
        score_head(qs0, 0, 0, hd)
    lax.fori_loop(0, n_query_tiles, q_body, jnp.int32(0))


def _sb_attn_kernel(q_ref, k_ref, vt_ref, later_ref, o_ref, z_scr, later_scr, acc_scr, *, tq, tk, seq, heads):
    blk = later_ref.shape[1]
    n_query_tiles = seq // tq
    lane = lax.broadcasted_iota(jnp.int32, (tq, LANES), 1)
    key = lax.broadcasted_iota(jnp.int32, (CHUNK, tq), 0)
    qry = lax.broadcasted_iota(jnp.int32, (CHUNK, tq), 1)

    def load_queries(qi):
        r0 = pl.multiple_of(qi * tq, tq)
        qs = []
        for hd in range(heads):
            pair = q_ref[pl.ds(r0, tq), (hd // 2) * LANES:(hd // 2 + 1) * LANES]
            mine = (lane < B_DIM) if hd % 2 == 0 else (lane >= B_DIM)
            qs.append(jnp.where(mine, pair, jnp.zeros_like(pair)))
        return qs

    def logits(qs, j, slot, hd):
        c0 = pl.multiple_of(j * tk, tk)
        k = k_ref[pl.ds(c0, tk), (hd // 2) * LANES:(hd // 2 + 1) * LANES]
        z_scr[slot, hd] = _dot_nt(k, qs[hd])

    def q_body(qi, _):
        r0 = pl.multiple_of(qi * tq, tq)
        qs = load_queries(qi)
        diag = (qi * tq) // tk

        def cumulative_sums(slot, hd, b, strict_limit):
            parts = []
            for c in range(blk // CHUNK):
                rows = slice(b * blk + c * CHUNK, b * blk + (c + 1) * CHUNK)
                z = z_scr[slot, hd, rows, :]
                neg_lse = jnp.log(1.0 + jnp.exp2(-jnp.abs(z))) * (-LOG2E)
                log_rest = neg_lse - jnp.maximum(z, 0.0)
                z_scr[slot, hd, rows, :] = log_rest + z
                if strict_limit is not None:
                    log_rest = jnp.where(key + (b * blk + c * CHUNK) < strict_limit, log_rest, 0.0)
                parts.append(log_rest.astype(BF16))
            return _dot(later_ref[...], jnp.concatenate(parts, axis=0))

        def weigh_values(j, slot, hd, b, sums, later, strict_limit):
            weights = []
            for c in range(blk // CHUNK):
                rows = slice(b * blk + c * CHUNK, b * blk + (c + 1) * CHUNK)
                a = jnp.exp2(z_scr[slot, hd, rows, :] + sums[c * CHUNK:(c + 1) * CHUNK, :] + later)
                if strict_limit is not None:
                    a = jnp.where(key + (b * blk + c * CHUNK) < strict_limit, a, 0.0)
                weights.append(a.astype(BF16))
            vt = vt_ref[j, hd * B_DIM:(hd + 1) * B_DIM, b * blk:(b + 1) * blk]
            return later + sums[blk:blk + 1], _dot(vt, jnp.concatenate(weights, axis=0))

        def tile(j, slot, strict_limit=None, ahead=None, recycle=None):
            blocks = list(reversed(range(tk // blk)))
            pending = None
            for hd in range(heads + 1):
                sums = [cumulative_sums(slot, hd, b, strict_limit) for b in blocks] if hd < heads else None
                if ahead is not None and hd < heads:
                    logits(qs, ahead[0], ahead[1], hd)
                if pending is not None:
                    later = later_scr[hd - 1]
                    for b, block_sums in zip(blocks, pending):
                        later, weighted = weigh_values(j, slot, hd - 1, b, block_sums, later, strict_limit)
                        acc_scr[hd - 1] += weighted
                    later_scr[hd - 1] = later
                    if recycle is not None:
                        logits(recycle[0], recycle[1], slot, hd - 1)
                pending = sums

        def any_weight_left():
            worst = later_scr[0]
            for hd in range(1, heads):
                worst = jnp.maximum(worst, later_scr[hd])
            return (jnp.max(worst) > F32_UNDERFLOW_LOG2).astype(jnp.int32)

        def more(state):
            n, live = state
            return jnp.logical_and(n <= diag, live > 0)

        def body(state):
            n, _ = state
            j = diag - n

            @pl.when(n >= 2)
            def _():
                for hd in range(heads):
                    logits(qs, j, 1, hd)

            tile(j, 1)
            return n + 1, any_weight_left()

        for hd in range(heads):
            later_scr[hd] = jnp.zeros((1, tq), F32)
            acc_scr[hd] = jnp.zeros((B_DIM, tq), F32)
        qi_next = jnp.minimum(qi + 1, n_query_tiles - 1)
        tile(diag, 0, qry + (r0 - diag * tk), ahead=(jnp.maximum(diag - 1, 0), 1),
             recycle=(load_queries(qi_next), (qi_next * tq) // tk))
        lax.while_loop(more, body, (jnp.int32(1), any_weight_left()))
        o_ref[pl.ds(r0, tq), :] = jnp.concatenate([acc_scr[hd] for hd in range(heads)], axis=0).T
        return 0

    qs0 = load_queries(0)
    for hd in range(heads):
        logits(qs0, 0, 0, hd)
    lax.fori_loop(0, n_query_tiles, q_body, 0)


def _attn_params():
    return pltpu.CompilerParams(dimension_semantics=("parallel", "parallel"), vmem_limit_bytes=VMEM_LIMIT)


def _mla_attn(q, k, vt, tq):
    batch, seq, _ = q.shape
    tk = vt.shape[-1]
    heads = MLA_HEADS_PER_STEP
    qk_spec = pl.BlockSpec((None, seq, heads * LANES), lambda b, g: (b, 0, g))
    return pl.pallas_call(
        functools.partial(_mla_attn_kernel, tq=tq, tk=tk, seq=seq, heads=heads),
        grid=(batch, A_HEADS // heads),
        in_specs=[qk_spec, qk_spec,
                  pl.BlockSpec((None, seq // tk, heads * A_VROWS, tk), lambda b, g: (b, 0, g, 0))],
        out_specs=pl.BlockSpec((None, seq, heads * A_VDIM), lambda b, g: (b, 0, g)),
        out_shape=jax.ShapeDtypeStruct((batch, seq, A_HEADS * A_VDIM), F32),
        scratch_shapes=[pltpu.VMEM((2, heads, tk, tq), F32), pltpu.VMEM((2, heads, 1, tq), F32),
                        pltpu.VMEM((heads, 1, tq), F32), pltpu.VMEM((heads, A_VROWS, tq), F32)],
        compiler_params=_attn_params(),
        name="mla_attn",
    )(q, k, vt)


def _sb_attn(q, k, vt, later2, tq):
    batch, seq, _ = q.shape
    tk = vt.shape[-1]
    heads = SB_HEADS_PER_STEP
    spec = pl.BlockSpec((None, seq, heads * B_DIM), lambda b, g: (b, 0, g))
    return pl.pallas_call(
        functools.partial(_sb_attn_kernel, tq=tq, tk=tk, seq=seq, heads=heads),
        grid=(batch, B_HEADS // heads),
        in_specs=[spec, spec, pl.BlockSpec((None, seq // tk, heads * B_DIM, tk), lambda b, g: (b, 0, g, 0)),
                  pl.BlockSpec(later2.shape, lambda b, g: (0, 0))],
        out_specs=spec,
        out_shape=jax.ShapeDtypeStruct((batch, seq, B_HEADS * B_DIM), F32),
        scratch_shapes=[pltpu.VMEM((2, heads, tk, tq), F32), pltpu.VMEM((heads, 1, tq), F32),
                        pltpu.VMEM((heads, B_DIM, tq), F32)],
        compiler_params=_attn_params(),
        name="sb_attn",
    )(q, k, vt, later2)


def _rope_tables(seq):
    pos = jnp.arange(seq, dtype=F32)
    inv = 1.0 / (ROPE_THETA ** (jnp.arange(0, A_ROPE, 2, dtype=F32) / A_ROPE))
    ang = pos[:, None] * inv[None, :]
    cos, sin = jnp.cos(ang), jnp.sin(ang)
    ones = jnp.ones((seq, A_NOPE), F32)
    zeros_n = jnp.zeros((seq, A_NOPE), F32)
    zeros_p = jnp.zeros((seq, LANES - A_NOPE - A_ROPE), F32)
    cos_t = jnp.concatenate([ones, cos, cos, zeros_p], axis=1)
    sin_t = jnp.concatenate([zeros_n, sin, sin, zeros_p], axis=1)
    return cos_t, sin_t


def _rot_half_cols(w):
    half = A_ROPE // 2
    return jnp.concatenate([-w[..., half:], w[..., :half]], axis=-1)


def _mla_weights(w_in, w_uq, w_ukv):
    d = w_in.shape[0]
    pad = LANES - A_NOPE - A_ROPE
    c_q = w_in[:, :A_QLORA]
    c_kv = w_in[:, A_QLORA:A_QLORA + A_KVLORA]
    k_r = w_in[:, A_QLORA + A_KVLORA:A_QLORA + A_KVLORA + A_ROPE]
    w_gate = w_in[:, A_QLORA + A_KVLORA + A_ROPE:]
    assert pad == A_ROPE
    zn = jnp.zeros((d, A_NOPE), F32)
    w_in_p = jnp.concatenate([c_q, c_kv, zn, k_r, _rot_half_cols(k_r)], axis=1)

    wq = w_uq.reshape(A_QLORA, A_HEADS, A_NOPE + A_ROPE)
    nope, rope = wq[..., :A_NOPE], wq[..., A_NOPE:]
    w_q_p = jnp.concatenate([nope, rope, _rot_half_cols(rope)], axis=-1).reshape(A_QLORA, A_HEADS * LANES)

    wkv = w_ukv.reshape(A_KVLORA, A_HEADS, A_NOPE + A_VDIM)
    kn, vv = wkv[..., :A_NOPE], wkv[..., A_NOPE:]
    zk = jnp.zeros((A_KVLORA, A_HEADS, LANES - A_NOPE), F32)
    w_k_p = jnp.concatenate([kn, zk], axis=-1).reshape(A_KVLORA, A_HEADS * LANES)
    w_vt = vv.reshape(A_KVLORA, A_HEADS * A_VDIM).T
    return (w_in_p.astype(BF16), w_gate.astype(BF16), w_q_p.astype(BF16), w_k_p.astype(BF16),
            w_vt.astype(BF16))


def _later_key_matrix(blk):
    s = lax.broadcasted_iota(jnp.int32, (blk, blk), 0)
    j = lax.broadcasted_iota(jnp.int32, (blk, blk), 1)
    return jnp.concatenate([(j > s).astype(BF16), jnp.ones((BF16_SUBLANES, blk), BF16)], axis=0)


def kernel(x, a_norm_pre, a_w_in, a_q_norm, a_w_uq, a_kv_norm, a_w_ukv, a_w_o, a_norm_post,
           b_kv_norm, b_w_kv, b_norm_pre, b_w_in, b_w_o, b_norm_post):
    batch, seq, d = x.shape
    rows = batch * seq
    tq = min(ATTN_TQ, seq)
    tk = min(MLA_TK, seq)
    x2 = x.reshape(rows, d)
    cos_t, sin_t = _rope_tables(seq)
    n_a = a_w_in.shape[0]
    n_b = b_w_in.shape[0]

    for i in range(n_a):
        w_in_p, w_gate, w_q_p, w_k_p, w_vt = _mla_weights(a_w_in[i], a_w_uq[i], a_w_ukv[i])
        q, k, vt = _mla_proj(x2, a_norm_pre[i][None], w_in_p, a_q_norm[i][None], w_q_p,
                             a_kv_norm[i][None], w_k_p, w_vt, cos_t, sin_t, seq, tk)
        o = _mla_attn(q.reshape(batch, seq, -1), k.reshape(batch, seq, -1), vt, tq)
        x2 = _out_proj(x2, o.reshape(rows, -1), a_norm_pre[i][None], w_gate,
                       a_w_o[i].astype(BF16), a_norm_post[i][None])

    width = B_HEADS * B_DIM
    tk = min(SB_TK, seq)
    later2 = _later_key_matrix(min(SB_BLOCK, tk))
    k = vt = None
    for j in range(n_b):
        w_q = b_w_in[j][:, :width].astype(BF16)
        w_gate = b_w_in[j][:, width:].astype(BF16)
        if j == 0:
            q, k, vt = _sb_proj(x2, b_norm_pre[j][None], w_q, seq, tk, b_kv_norm[None],
                                b_w_kv[:, :width].astype(BF16), b_w_kv[:, width:].T.astype(BF16))
            k = k.reshape(batch, seq, width)
        else:
            (q,) = _sb_proj(x2, b_norm_pre[j][None], w_q, seq, tk)
        o = _sb_attn(q.reshape(batch, seq, width), k, vt, later2, tq)
        x2 = _out_proj(x2, o.reshape(rows, width), b_norm_pre[j][None], w_gate,
                       b_w_o[j].astype(BF16), b_norm_post[j][None])
    return x2.reshape(batch, seq, d)
```

```python
import functools
import math

import jax
import jax.numpy as jnp
from jax import lax
from jax.experimental import pallas as pl
from jax.experimental.pallas import tpu as pltpu

D_MODEL = 1024
A_HEADS = 16
A_NOPE = 64
A_ROPE = 32
A_VDIM = 64
A_QLORA = 256
A_KVLORA = 128
B_HEADS = 16
B_DIM = 64
ROPE_THETA = 10000.0
EPS = 1e-6

LANES = 128
ROW_TILE = 512
OUT_ROW_TILE = 1024
ROW_PARTS = 4
BF16_SUBLANES = 16
A_VROWS = A_VDIM + BF16_SUBLANES
ATTN_TQ = 256
MLA_TK = 512
MLA_HEADS_PER_STEP = 4
PREFETCH_HEADS = 2
SB_TK = 256
SB_HEADS_PER_STEP = 8
SB_BLOCK = 128
F32_UNDERFLOW_LOG2 = -152.0
CHUNK = 32
VMEM_LIMIT = 56 * 1024 * 1024
LOG2E = 1.4426950408889634

F32 = jnp.float32
BF16 = jnp.bfloat16


def _rms_scale(x):
    return lax.rsqrt(jnp.mean(x * x, axis=-1, keepdims=True) + EPS)


def _dot(a, b):
    return jnp.dot(a, b, preferred_element_type=F32)


def _dot_nt(a, b):
    return lax.dot_general(a, b, (((1,), (1,)), ((), ())), preferred_element_type=F32)


def _store_key_tiles(vt_ref, v_t, ones_rows=0):
    tile = vt_ref.shape[-1]
    for t in range(vt_ref.shape[0]):
        cols = slice(t * tile, (t + 1) * tile)
        if not ones_rows:
            vt_ref[t] = v_t[:, cols].astype(BF16)
            continue
        per_head = A_VDIM + ones_rows
        for hd in range(v_t.shape[0] // A_VDIM):
            vt_ref[t, hd * per_head:hd * per_head + A_VDIM, :] = v_t[hd * A_VDIM:(hd + 1) * A_VDIM, cols].astype(BF16)
            vt_ref[t, hd * per_head + A_VDIM:(hd + 1) * per_head, :] = jnp.ones((ones_rows, tile), BF16)


def _mla_proj_kernel(x_ref, g_ref, win_ref, qg_ref, wq_ref, kvg_ref, wk_ref, wvt_ref, cos_ref, sin_ref,
                     q_ref, k_ref, vt_ref):
    x = x_ref[...]
    h = x * _rms_scale(x) * g_ref[...]
    proj = _dot(h.astype(BF16), win_ref[...])
    cq = proj[:, :A_QLORA]
    ckv = proj[:, A_QLORA:A_QLORA + A_KVLORA]
    kr_blk = proj[:, A_QLORA + A_KVLORA:]
    cqn = cq * _rms_scale(cq) * qg_ref[...]
    q2 = _dot(cqn.astype(BF16), wq_ref[...])
    ckvn = (ckv * _rms_scale(ckv) * kvg_ref[...]).astype(BF16)
    kn = _dot(ckvn, wk_ref[...])
    _store_key_tiles(vt_ref, _dot_nt(wvt_ref[...], ckvn), BF16_SUBLANES)
    cos = cos_ref[...]
    sin = sin_ref[...]
    kr = kr_blk * cos + pltpu.roll(kr_blk, LANES - A_ROPE, 1) * sin
    scale = LOG2E / math.sqrt(A_NOPE + A_ROPE)
    cos_q = cos * scale
    sin_q = sin * scale
    for hd in range(A_HEADS):
        sl = slice(hd * LANES, (hd + 1) * LANES)
        q_blk = q2[:, sl]
        q_ref[:, sl] = (q_blk * cos_q + pltpu.roll(q_blk, LANES - A_ROPE, 1) * sin_q).astype(BF16)
        k_ref[:, sl] = (kn[:, sl] + kr).astype(BF16)


def _sb_proj_kernel(*refs, with_kv):
    if with_kv:
        x_ref, gq_ref, wq_ref, gkv_ref, wk_ref, wvt_ref, q_ref, k_ref, vt_ref = refs
    else:
        x_ref, gq_ref, wq_ref, q_ref = refs
    x = x_ref[...]
    y = x * _rms_scale(x)
    q = _dot((y * gq_ref[...]).astype(BF16), wq_ref[...])
    q_ref[...] = (q * (LOG2E / math.sqrt(B_DIM))).astype(BF16)
    if with_kv:
        h_kv = (y * gkv_ref[...]).astype(BF16)
        k_ref[...] = _dot(h_kv, wk_ref[...]).astype(BF16)
        _store_key_tiles(vt_ref, _dot_nt(wvt_ref[...], h_kv))


def _row_parts(tm):
    part = tm // ROW_PARTS
    return [slice(i * part, (i + 1) * part) for i in range(ROW_PARTS)]


def _out_kernel(x_ref, o_ref, gpre_ref, wg_ref, wo_ref, gpost_ref, xn_ref):
    parts = _row_parts(x_ref.shape[0])
    gates = []
    for rows in parts:
        x = x_ref[rows, :]
        h = x * _rms_scale(x) * gpre_ref[...]
        gates.append(_dot(h.astype(BF16), wg_ref[...]))
    outs = []
    for rows, gate in zip(parts, gates):
        og = o_ref[rows, :] * (gate * (1.0 / (1.0 + jnp.exp(-gate))))
        outs.append(_dot(og.astype(BF16), wo_ref[...]))
    for rows, out in zip(parts, outs):
        xn_ref[rows, :] = x_ref[rows, :] + out * _rms_scale(out) * gpost_ref[...]


def _row_spec(tm, width):
    return pl.BlockSpec((tm, width), lambda i: (i, 0))


def _full_spec(shape):
    return pl.BlockSpec(shape, lambda i: (0,) * len(shape))


def _row_params():
    return pltpu.CompilerParams(dimension_semantics=("parallel",), vmem_limit_bytes=VMEM_LIMIT)


def _vt_out(rows, seq, tm, tile, width):
    tiles_per_seq = seq // tm
    spec = pl.BlockSpec((None, tm // tile, width, tile), lambda i: (i // tiles_per_seq, i % tiles_per_seq, 0, 0))
    return spec, jax.ShapeDtypeStruct((rows // seq, seq // tile, width, tile), BF16)


def _mla_proj(x2, g, w_in_p, qg, w_q_p, kvg, w_k_p, w_vt, cos_t, sin_t, seq, tile):
    rows = x2.shape[0]
    tm = min(ROW_TILE, seq)
    tiles_per_seq = seq // tm
    hw = A_HEADS * LANES
    table_spec = pl.BlockSpec((tm, LANES), lambda i: (i % tiles_per_seq, 0))
    vt_spec, vt_shape = _vt_out(rows, seq, tm, tile, A_HEADS * A_VROWS)
    return pl.pallas_call(
        _mla_proj_kernel,
        grid=(rows // tm,),
        in_specs=[_row_spec(tm, D_MODEL), _full_spec(g.shape), _full_spec(w_in_p.shape),
                  _full_spec(qg.shape), _full_spec(w_q_p.shape), _full_spec(kvg.shape),
                  _full_spec(w_k_p.shape), _full_spec(w_vt.shape), table_spec, table_spec],
        out_specs=[_row_spec(tm, hw), _row_spec(tm, hw), vt_spec],
        out_shape=[jax.ShapeDtypeStruct((rows, hw), BF16), jax.ShapeDtypeStruct((rows, hw), BF16), vt_shape],
        compiler_params=_row_params(),
        name="mla_proj",
    )(x2, g, w_in_p, qg, w_q_p, kvg, w_k_p, w_vt, cos_t, sin_t)


def _sb_proj(x2, gq, w_q, seq, tile, gkv=None, w_k=None, w_vt=None):
    rows = x2.shape[0]
    tm = min(ROW_TILE, seq)
    width = B_HEADS * B_DIM
    with_kv = w_k is not None
    args = [x2, gq, w_q] + ([gkv, w_k, w_vt] if with_kv else [])
    in_specs = [_row_spec(tm, D_MODEL)] + [_full_spec(a.shape) for a in args[1:]]
    out_specs = [_row_spec(tm, width)]
    out_shape = [jax.ShapeDtypeStruct((rows, width), BF16)]
    if with_kv:
        vt_spec, vt_shape = _vt_out(rows, seq, tm, tile, width)
        out_specs += [_row_spec(tm, width), vt_spec]
        out_shape += [jax.ShapeDtypeStruct((rows, width), BF16), vt_shape]
    return pl.pallas_call(
        functools.partial(_sb_proj_kernel, with_kv=with_kv),
        grid=(rows // tm,),
        in_specs=in_specs,
        out_specs=out_specs,
        out_shape=out_shape,
        compiler_params=_row_params(),
        name="sb_proj_kv" if with_kv else "sb_proj",
    )(*args)


def _out_proj(x2, o2, gpre, w_gate, w_o, gpost):
    rows = x2.shape[0]
    tm = min(OUT_ROW_TILE, rows)
    return pl.pallas_call(
        _out_kernel,
        grid=(rows // tm,),
        in_specs=[_row_spec(tm, D_MODEL), _row_spec(tm, o2.shape[1]), _full_spec(gpre.shape),
                  _full_spec(w_gate.shape), _full_spec(w_o.shape), _full_spec(gpost.shape)],
        out_specs=_row_spec(tm, D_MODEL),
        out_shape=jax.ShapeDtypeStruct((rows, D_MODEL), F32),
        compiler_params=_row_params(),
        name="out_proj",
    )(x2, o2, gpre, w_gate, w_o, gpost)


def _mla_attn_kernel(q_ref, k_ref, vt_ref, o_ref, s_scr, max_scr, m_scr, acc_scr, *, tq, tk, seq, heads):
    n_query_tiles = seq // tq
    key = lax.broadcasted_iota(jnp.int32, (CHUNK, tq), 0)
    qry = lax.broadcasted_iota(jnp.int32, (CHUNK, tq), 1)

    def load_queries(qi):
        r0 = pl.multiple_of(qi * tq, tq)
        return [q_ref[pl.ds(r0, tq), hd * LANES:(hd + 1) * LANES] for hd in range(heads)]

    def score_head(qs, j, slot, hd):
        c0 = pl.multiple_of(j * tk, tk)
        s = _dot_nt(k_ref[pl.ds(c0, tk), hd * LANES:(hd + 1) * LANES], qs[hd])
        s_scr[slot, hd] = s
        max_scr[slot, hd] = jnp.max(s, axis=0, keepdims=True)

    def score_chunk(slot, hd, c, diag_offset):
        blk = s_scr[slot, hd, c * CHUNK:(c + 1) * CHUNK, :]
        if diag_offset is None or (c + 1) * CHUNK - 1 <= diag_offset:
            return blk
        return jnp.where(key + (c * CHUNK - diag_offset) <= qry, blk, -jnp.inf)

    def tile(j, slot, prefetch, diag_offset=None):
        n_chunks = tk // CHUNK if diag_offset is None else (diag_offset + tq) // CHUNK
        pending = None
        for hd in range(heads):
            for ahead in range(PREFETCH_HEADS * hd, min(PREFETCH_HEADS * (hd + 1), heads)):
                prefetch(ahead)
            if diag_offset is None:
                mx = max_scr[slot, hd]
            else:
                mx = score_chunk(slot, hd, 0, diag_offset)
                for c in range(1, n_chunks):
                    mx = jnp.maximum(mx, score_chunk(slot, hd, c, diag_offset))
                mx = jnp.max(mx, axis=0, keepdims=True)
            m = m_scr[hd]
            m_new = jnp.maximum(m, mx)
            m_scr[hd] = m_new
            alpha = jnp.exp2(m - m_new)
            ps = [jnp.exp2(score_chunk(slot, hd, c, diag_offset) - m_new).astype(BF16)
                  for c in range(n_chunks)]
            pv = _dot(vt_ref[j, hd * A_VROWS:(hd + 1) * A_VROWS, :n_chunks * CHUNK], jnp.concatenate(ps, axis=0))
            if pending is not None:
                acc_scr[pending[0]] = pending[1] * acc_scr[pending[0]] + pending[2]
            pending = (hd, alpha, pv)
        acc_scr[pending[0]] = pending[1] * acc_scr[pending[0]] + pending[2]

    def q_body(qi, first_buffer):
        r0 = pl.multiple_of(qi * tq, tq)
        qs = load_queries(qi)
        diag = (qi * tq) // tk
        for hd in range(heads):
            m_scr[hd] = jnp.full((1, tq), -jnp.inf, F32)
            acc_scr[hd] = jnp.zeros((A_VROWS, tq), F32)

        def body(j, _):
            for slot in range(2):
                @pl.when((j + first_buffer) % 2 == slot)
                def _():
                    tile(j, slot, lambda hd: score_head(qs, j + 1, 1 - slot, hd))
            return 0

        lax.fori_loop(0, diag, body, 0)
        diag_buffer = (diag + first_buffer) % 2
        position = (r0 - diag * tk) // tq
        for slot in range(2):
            for pos in range(tk // tq):
                @pl.when(jnp.logical_and(diag_buffer == slot, position == pos))
                def _():
                    qs_next = load_queries(jnp.minimum(qi + 1, n_query_tiles - 1))
                    tile(diag, slot, lambda hd: score_head(qs_next, 0, 1 - slot, hd), pos * tq)
                    outs = [acc_scr[hd, :A_VDIM, :] / acc_scr[hd, A_VDIM:A_VDIM + 1, :] for hd in range(heads)]
                    o_ref[pl.ds(r0, tq), :] = jnp.concatenate(outs, axis=0).T
        return 1 - diag_buffer

    qs0 = load_queries(0)
    for hd in range(heads):
        score_head(qs0, 0, 0, hd)
    lax.fori_loop(0, n_query_tiles, q_body, jnp.int32(0))


def _sb_attn_kernel(q_ref, k_ref, vt_ref, later_ref, o_ref, z_scr, later_scr, acc_scr, *, tq, tk, seq, heads):
    blk = later_ref.shape[1]
    n_query_tiles = seq // tq
    lane = lax.broadcasted_iota(jnp.int32, (tq, LANES), 1)
    key = lax.broadcasted_iota(jnp.int32, (CHUNK, tq), 0)
    qry = lax.broadcasted_iota(jnp.int32, (CHUNK, tq), 1)

    def load_queries(qi):
        r0 = pl.multiple_of(qi * tq, tq)
        qs = []
        for hd in range(heads):
            pair = q_ref[pl.ds(r0, tq), (hd // 2) * LANES:(hd // 2 + 1) * LANES]
            mine = (lane < B_DIM) if hd % 2 == 0 else (lane >= B_DIM)
            qs.append(jnp.where(mine, pair, jnp.zeros_like(pair)))
        return qs

    def logits(qs, j, slot, hd):
        c0 = pl.multiple_of(j * tk, tk)
        k = k_ref[pl.ds(c0, tk), (hd // 2) * LANES:(hd // 2 + 1) * LANES]
        z_scr[slot, hd] = _dot_nt(k, qs[hd])

    def q_body(qi, _):
        r0 = pl.multiple_of(qi * tq, tq)
        qs = load_queries(qi)
        diag = (qi * tq) // tk

        def cumulative_sums(slot, hd, b, strict_limit):
            parts = []
            for c in range(blk // CHUNK):
                rows = slice(b * blk + c * CHUNK, b * blk + (c + 1) * CHUNK)
                z = z_scr[slot, hd, rows, :]
                neg_lse = jnp.log(1.0 + jnp.exp2(-jnp.abs(z))) * (-LOG2E)
                log_rest = neg_lse - jnp.maximum(z, 0.0)
                z_scr[slot, hd, rows, :] = log_rest + z
                if strict_limit is not None:
                    log_rest = jnp.where(key + (b * blk + c * CHUNK) < strict_limit, log_rest, 0.0)
                parts.append(log_rest.astype(BF16))
            return _dot(later_ref[...], jnp.concatenate(parts, axis=0))

        def weigh_values(j, slot, hd, b, sums, later, strict_limit):
            weights = []
            for c in range(blk // CHUNK):
                rows = slice(b * blk + c * CHUNK, b * blk + (c + 1) * CHUNK)
                a = jnp.exp2(z_scr[slot, hd, rows, :] + sums[c * CHUNK:(c + 1) * CHUNK, :] + later)
                if strict_limit is not None:
                    a = jnp.where(key + (b * blk + c * CHUNK) < strict_limit, a, 0.0)
                weights.append(a.astype(BF16))
            vt = vt_ref[j, hd * B_DIM:(hd + 1) * B_DIM, b * blk:(b + 1) * blk]
            return later + sums[blk:blk + 1], _dot(vt, jnp.concatenate(weights, axis=0))

        def tile(j, slot, strict_limit=None, ahead=None, recycle=None):
            blocks = list(reversed(range(tk // blk)))
            pending = None
            for hd in range(heads + 1):
                sums = [cumulative_sums(slot, hd, b, strict_limit) for b in blocks] if hd < heads else None
                if ahead is not None and hd < heads:
                    logits(qs, ahead[0], ahead[1], hd)
                if pending is not None:
                    later = later_scr[hd - 1]
                    for b, block_sums in zip(blocks, pending):
                        later, weighted = weigh_values(j, slot, hd - 1, b, block_sums, later, strict_limit)
                        acc_scr[hd - 1] += weighted
                    later_scr[hd - 1] = later
                    if recycle is not None:
                        logits(recycle[0], recycle[1], slot, hd - 1)
                pending = sums

        def any_weight_left():
            worst = later_scr[0]
            for hd in range(1, heads):
                worst = jnp.maximum(worst, later_scr[hd])
            return (jnp.max(worst) > F32_UNDERFLOW_LOG2).astype(jnp.int32)

        def more(state):
            n, live = state
            return jnp.logical_and(n <= diag, live > 0)

        def body(state):
            n, _ = state
            j = diag - n

            @pl.when(n >= 2)
            def _():
                for hd in range(heads):
                    logits(qs, j, 1, hd)

            tile(j, 1)
            return n + 1, any_weight_left()

        for hd in range(heads):
            later_scr[hd] = jnp.zeros((1, tq), F32)
            acc_scr[hd] = jnp.zeros((B_DIM, tq), F32)
        qi_next = jnp.minimum(qi + 1, n_query_tiles - 1)
        tile(diag, 0, qry + (r0 - diag * tk), ahead=(jnp.maximum(diag - 1, 0), 1),
             recycle=(load_queries(qi_next), (qi_next * tq) // tk))
        lax.while_loop(more, body, (jnp.int32(1), any_weight_left()))
        o_ref[pl.ds(r0, tq), :] = jnp.concatenate([acc_scr[hd] for hd in range(heads)], axis=0).T
        return 0

    qs0 = load_queries(0)
    for hd in range(heads):
        logits(qs0, 0, 0, hd)
    lax.fori_loop(0, n_query_tiles, q_body, 0)


def _attn_params():
    return pltpu.CompilerParams(dimension_semantics=("parallel", "parallel"), vmem_limit_bytes=VMEM_LIMIT)


def _mla_attn(q, k, vt, tq):
    batch, seq, _ = q.shape
    tk = vt.shape[-1]
    heads = MLA_HEADS_PER_STEP
    qk_spec = pl.BlockSpec((None, seq, heads * LANES), lambda b, g: (b, 0, g))
    return pl.pallas_call(
        functools.partial(_mla_attn_kernel, tq=tq, tk=tk, seq=seq, heads=heads),
        grid=(batch, A_HEADS // heads),
        in_specs=[qk_spec, qk_spec,
                  pl.BlockSpec((None, seq // tk, heads * A_VROWS, tk), lambda b, g: (b, 0, g, 0))],
        out_specs=pl.BlockSpec((None, seq, heads * A_VDIM), lambda b, g: (b, 0, g)),
        out_shape=jax.ShapeDtypeStruct((batch, seq, A_HEADS * A_VDIM), F32),
        scratch_shapes=[pltpu.VMEM((2, heads, tk, tq), F32), pltpu.VMEM((2, heads, 1, tq), F32),
                        pltpu.VMEM((heads, 1, tq), F32), pltpu.VMEM((heads, A_VROWS, tq), F32)],
        compiler_params=_attn_params(),
        name="mla_attn",
    )(q, k, vt)


def _sb_attn(q, k, vt, later2, tq):
    batch, seq, _ = q.shape
    tk = vt.shape[-1]
    heads = SB_HEADS_PER_STEP
    spec = pl.BlockSpec((None, seq, heads * B_DIM), lambda b, g: (b, 0, g))
    return pl.pallas_call(
        functools.partial(_sb_attn_kernel, tq=tq, tk=tk, seq=seq, heads=heads),
        grid=(batch, B_HEADS // heads),
        in_specs=[spec, spec, pl.BlockSpec((None, seq // tk, heads * B_DIM, tk), lambda b, g: (b, 0, g, 0)),
                  pl.BlockSpec(later2.shape, lambda b, g: (0, 0))],
        out_specs=spec,
        out_shape=jax.ShapeDtypeStruct((batch, seq, B_HEADS * B_DIM), F32),
        scratch_shapes=[pltpu.VMEM((2, heads, tk, tq), F32), pltpu.VMEM((heads, 1, tq), F32),
                        pltpu.VMEM((heads, B_DIM, tq), F32)],
        compiler_params=_attn_params(),
        name="sb_attn",
    )(q, k, vt, later2)


def _rope_tables(seq):
    pos = jnp.arange(seq, dtype=F32)
    inv = 1.0 / (ROPE_THETA ** (jnp.arange(0, A_ROPE, 2, dtype=F32) / A_ROPE))
    ang = pos[:, None] * inv[None, :]
    cos, sin = jnp.cos(ang), jnp.sin(ang)
    ones = jnp.ones((seq, A_NOPE), F32)
    zeros_n = jnp.zeros((seq, A_NOPE), F32)
    zeros_p = jnp.zeros((seq, LANES - A_NOPE - A_ROPE), F32)
    cos_t = jnp.concatenate([ones, cos, cos, zeros_p], axis=1)
    sin_t = jnp.concatenate([zeros_n, sin, sin, zeros_p], axis=1)
    return cos_t, sin_t


def _rot_half_cols(w):
    half = A_ROPE // 2
    return jnp.concatenate([-w[..., half:], w[..., :half]], axis=-1)


def _mla_weights(w_in, w_uq, w_ukv):
    d = w_in.shape[0]
    pad = LANES - A_NOPE - A_ROPE
    c_q = w_in[:, :A_QLORA]
    c_kv = w_in[:, A_QLORA:A_QLORA + A_KVLORA]
    k_r = w_in[:, A_QLORA + A_KVLORA:A_QLORA + A_KVLORA + A_ROPE]
    w_gate = w_in[:, A_QLORA + A_KVLORA + A_ROPE:]
    assert pad == A_ROPE
    zn = jnp.zeros((d, A_NOPE), F32)
    w_in_p = jnp.concatenate([c_q, c_kv, zn, k_r, _rot_half_cols(k_r)], axis=1)

    wq = w_uq.reshape(A_QLORA, A_HEADS, A_NOPE + A_ROPE)
    nope, rope = wq[..., :A_NOPE], wq[..., A_NOPE:]
    w_q_p = jnp.concatenate([nope, rope, _rot_half_cols(rope)], axis=-1).reshape(A_QLORA, A_HEADS * LANES)

    wkv = w_ukv.reshape(A_KVLORA, A_HEADS, A_NOPE + A_VDIM)
    kn, vv = wkv[..., :A_NOPE], wkv[..., A_NOPE:]
    zk = jnp.zeros((A_KVLORA, A_HEADS, LANES - A_NOPE), F32)
    w_k_p = jnp.concatenate([kn, zk], axis=-1).reshape(A_KVLORA, A_HEADS * LANES)
    w_vt = vv.reshape(A_KVLORA, A_HEADS * A_VDIM).T
    return (w_in_p.astype(BF16), w_gate.astype(BF16), w_q_p.astype(BF16), w_k_p.astype(BF16),
            w_vt.astype(BF16))


def _later_key_matrix(blk):
    s = lax.broadcasted_iota(jnp.int32, (blk, blk), 0)
    j = lax.broadcasted_iota(jnp.int32, (blk, blk), 1)
    return jnp.concatenate([(j > s).astype(BF16), jnp.ones((BF16_SUBLANES, blk), BF16)], axis=0)


def kernel(x, a_norm_pre, a_w_in, a_q_norm, a_w_uq, a_kv_norm, a_w_ukv, a_w_o, a_norm_post,
           b_kv_norm, b_w_kv, b_norm_pre, b_w_in, b_w_o, b_norm_post):
    batch, seq, d = x.shape
    rows = batch * seq
    tq = min(ATTN_TQ, seq)
    tk = min(MLA_TK, seq)
    x2 = x.reshape(rows, d)
    cos_t, sin_t = _rope_tables(seq)
    n_a = a_w_in.shape[0]
    n_b = b_w_in.shape[0]

    for i in range(n_a):
        w_in_p, w_gate, w_q_p, w_k_p, w_vt = _mla_weights(a_w_in[i], a_w_uq[i], a_w_ukv[i])
        q, k, vt = _mla_proj(x2, a_norm_pre[i][None], w_in_p, a_q_norm[i][None], w_q_p,
                             a_kv_norm[i][None], w_k_p, w_vt, cos_t, sin_t, seq, tk)
        o = _mla_attn(q.reshape(batch, seq, -1), k.reshape(batch, seq, -1), vt, tq)
        x2 = _out_proj(x2, o.reshape(rows, -1), a_norm_pre[i][None], w_gate,
                       a_w_o[i].astype(BF16), a_norm_post[i][None])

    width = B_HEADS * B_DIM
    tk = min(SB_TK, seq)
    later2 = _later_key_matrix(min(SB_BLOCK, tk))
    k = vt = None
    for j in range(n_b):
        w_q = b_w_in[j][:, :width].astype(BF16)
        w_gate = b_w_in[j][:, width:].astype(BF16)
        if j == 0:
            q, k, vt = _sb_proj(x2, b_norm_pre[j][None], w_q, seq, tk, b_kv_norm[None],
                                b_w_kv[:, :width].astype(BF16), b_w_kv[:, width:].T.astype(BF16))
            k = k.reshape(batch, seq, width)
        else:
            (q,) = _sb_proj(x2, b_norm_pre[j][None], w_q, seq, tk)
        o = _sb_attn(q.reshape(batch, seq, width), k, vt, later2, tq)
        x2 = _out_proj(x2, o.reshape(rows, width), b_norm_pre[j][None], w_gate,
                       b_w_o[j].astype(BF16), b_norm_post[j][None])
    return x2.reshape(batch, seq, d)
```

```python
import functools
import math

import jax
import jax.numpy as jnp
from jax import lax
from jax.experimental import pallas as pl
from jax.experimental.pallas import tpu as pltpu

D_MODEL = 1024
A_HEADS = 16
A_NOPE = 64
A_ROPE = 32
A_VDIM = 64
A_QLORA = 256
A_KVLORA = 128
B_HEADS = 16
B_DIM = 64
ROPE_THETA = 10000.0
EPS = 1e-6

LANES = 128
ROW_TILE = 512
OUT_ROW_TILE = 1024
ROW_PARTS = 4
BF16_SUBLANES = 16
A_VROWS = A_VDIM + BF16_SUBLANES
ATTN_TQ = 256
MLA_TK = 512
MLA_HEADS_PER_STEP = 4
PREFETCH_HEADS = 2
SB_TK = 256
SB_HEADS_PER_STEP = 8
SB_BLOCK = 128
F32_UNDERFLOW_LOG2 = -152.0
CHUNK = 32
VMEM_LIMIT = 56 * 1024 * 1024
LOG2E = 1.4426950408889634

F32 = jnp.float32
BF16 = jnp.bfloat16


def _rms_scale(x):
    return lax.rsqrt(jnp.mean(x * x, axis=-1, keepdims=True) + EPS)


def _dot(a, b):
    return jnp.dot(a, b, preferred_element_type=F32)


def _dot_nt(a, b):
    return lax.dot_general(a, b, (((1,), (1,)), ((), ())), preferred_element_type=F32)


def _store_key_tiles(vt_ref, v_t, ones_rows=0):
    tile = vt_ref.shape[-1]
    for t in range(vt_ref.shape[0]):
        cols = slice(t * tile, (t + 1) * tile)
        if not ones_rows:
            vt_ref[t] = v_t[:, cols].astype(BF16)
            continue
        per_head = A_VDIM + ones_rows
        for hd in range(v_t.shape[0] // A_VDIM):
            vt_ref[t, hd * per_head:hd * per_head + A_VDIM, :] = v_t[hd * A_VDIM:(hd + 1) * A_VDIM, cols].astype(BF16)
            vt_ref[t, hd * per_head + A_VDIM:(hd + 1) * per_head, :] = jnp.ones((ones_rows, tile), BF16)


def _mla_proj_kernel(x_ref, g_ref, win_ref, qg_ref, wq_ref, kvg_ref, wk_ref, wvt_ref, cos_ref, sin_ref,
                     q_ref, k_ref, vt_ref):
    x = x_ref[...]
    h = x * _rms_scale(x) * g_ref[...]
    proj = _dot(h.astype(BF16), win_ref[...])
    cq = proj[:, :A_QLORA]
    ckv = proj[:, A_QLORA:A_QLORA + A_KVLORA]
    kr_blk = proj[:, A_QLORA + A_KVLORA:]
    cqn = cq * _rms_scale(cq) * qg_ref[...]
    q2 = _dot(cqn.astype(BF16), wq_ref[...])
    ckvn = (ckv * _rms_scale(ckv) * kvg_ref[...]).astype(BF16)
    kn = _dot(ckvn, wk_ref[...])
    _store_key_tiles(vt_ref, _dot_nt(wvt_ref[...], ckvn), BF16_SUBLANES)
    cos = cos_ref[...]
    sin = sin_ref[...]
    kr = kr_blk * cos + pltpu.roll(kr_blk, LANES - A_ROPE, 1) * sin
    scale = LOG2E / math.sqrt(A_NOPE + A_ROPE)
    cos_q = cos * scale
    sin_q = sin * scale
    for hd in range(A_HEADS):
        sl = slice(hd * LANES, (hd + 1) * LANES)
        q_blk = q2[:, sl]
        q_ref[:, sl] = (q_blk * cos_q + pltpu.roll(q_blk, LANES - A_ROPE, 1) * sin_q).astype(BF16)
        k_ref[:, sl] = (kn[:, sl] + kr).astype(BF16)


def _sb_proj_kernel(*refs, with_kv):
    if with_kv:
        x_ref, gq_ref, wq_ref, gkv_ref, wk_ref, wvt_ref, q_ref, k_ref, vt_ref = refs
    else:
        x_ref, gq_ref, wq_ref, q_ref = refs
    x = x_ref[...]
    y = x * _rms_scale(x)
    q = _dot((y * gq_ref[...]).astype(BF16), wq_ref[...])
    q_ref[...] = (q * (LOG2E / math.sqrt(B_DIM))).astype(BF16)
    if with_kv:
        h_kv = (y * gkv_ref[...]).astype(BF16)
        k_ref[...] = _dot(h_kv, wk_ref[...]).astype(BF16)
        _store_key_tiles(vt_ref, _dot_nt(wvt_ref[...], h_kv))


def _row_parts(tm):
    part = tm // ROW_PARTS
    return [slice(i * part, (i + 1) * part) for i in range(ROW_PARTS)]


def _out_kernel(x_ref, o_ref, gpre_ref, wg_ref, wo_ref, gpost_ref, *rest):
    xn_ref = rest[-1] if len(rest) == 1 else rest[2]
    parts = _row_parts(x_ref.shape[0])
    gates = []
    for rows in parts:
        x = x_ref[rows, :]
        h = x * _rms_scale(x) * gpre_ref[...]
        gates.append(_dot(h.astype(BF16), wg_ref[...]))
    outs = []
    for rows, gate in zip(parts, gates):
        og = o_ref[rows, :] * (gate * (1.0 / (1.0 + jnp.exp(-gate))))
        outs.append(_dot(og.astype(BF16), wo_ref[...]))
    new_rows = []
    for rows, out in zip(parts, outs):
        xn = x_ref[rows, :] + out * _rms_scale(out) * gpost_ref[...]
        xn_ref[rows, :] = xn
        new_rows.append(xn)
    if len(rest) > 1:
        next_gq_ref, next_wq_ref, _, next_q_ref = rest
        for rows, xn in zip(parts, new_rows):
            h = (xn * _rms_scale(xn) * next_gq_ref[...]).astype(BF16)
            q = _dot(h, next_wq_ref[...])
            next_q_ref[rows, :] = (q * (LOG2E / math.sqrt(B_DIM))).astype(BF16)


def _row_spec(tm, width):
    return pl.BlockSpec((tm, width), lambda i: (i, 0))


def _full_spec(shape):
    return pl.BlockSpec(shape, lambda i: (0,) * len(shape))


def _row_params():
    return pltpu.CompilerParams(dimension_semantics=("parallel",), vmem_limit_bytes=VMEM_LIMIT)


def _vt_out(rows, seq, tm, tile, width):
    tiles_per_seq = seq // tm
    spec = pl.BlockSpec((None, tm // tile, width, tile), lambda i: (i // tiles_per_seq, i % tiles_per_seq, 0, 0))
    return spec, jax.ShapeDtypeStruct((rows // seq, seq // tile, width, tile), BF16)


def _mla_proj(x2, g, w_in_p, qg, w_q_p, kvg, w_k_p, w_vt, cos_t, sin_t, seq, tile):
    rows = x2.shape[0]
    tm = min(ROW_TILE, seq)
    tiles_per_seq = seq // tm
    hw = A_HEADS * LANES
    table_spec = pl.BlockSpec((tm, LANES), lambda i: (i % tiles_per_seq, 0))
    vt_spec, vt_shape = _vt_out(rows, seq, tm, tile, A_HEADS * A_VROWS)
    return pl.pallas_call(
        _mla_proj_kernel,
        grid=(rows // tm,),
        in_specs=[_row_spec(tm, D_MODEL), _full_spec(g.shape), _full_spec(w_in_p.shape),
                  _full_spec(qg.shape), _full_spec(w_q_p.shape), _full_spec(kvg.shape),
                  _full_spec(w_k_p.shape), _full_spec(w_vt.shape), table_spec, table_spec],
        out_specs=[_row_spec(tm, hw), _row_spec(tm, hw), vt_spec],
        out_shape=[jax.ShapeDtypeStruct((rows, hw), BF16), jax.ShapeDtypeStruct((rows, hw), BF16), vt_shape],
        compiler_params=_row_params(),
        name="mla_proj",
    )(x2, g, w_in_p, qg, w_q_p, kvg, w_k_p, w_vt, cos_t, sin_t)


def _sb_proj(x2, gq, w_q, seq, tile, gkv=None, w_k=None, w_vt=None):
    rows = x2.shape[0]
    tm = min(ROW_TILE, seq)
    width = B_HEADS * B_DIM
    with_kv = w_k is not None
    args = [x2, gq, w_q] + ([gkv, w_k, w_vt] if with_kv else [])
    in_specs = [_row_spec(tm, D_MODEL)] + [_full_spec(a.shape) for a in args[1:]]
    out_specs = [_row_spec(tm, width)]
    out_shape = [jax.ShapeDtypeStruct((rows, width), BF16)]
    if with_kv:
        vt_spec, vt_shape = _vt_out(rows, seq, tm, tile, width)
        out_specs += [_row_spec(tm, width), vt_spec]
        out_shape += [jax.ShapeDtypeStruct((rows, width), BF16), vt_shape]
    return pl.pallas_call(
        functools.partial(_sb_proj_kernel, with_kv=with_kv),
        grid=(rows // tm,),
        in_specs=in_specs,
        out_specs=out_specs,
        out_shape=out_shape,
        compiler_params=_row_params(),
        name="sb_proj_kv" if with_kv else "sb_proj",
    )(*args)


def _out_proj(x2, o2, gpre, w_gate, w_o, gpost, next_gq=None, next_wq=None):
    rows = x2.shape[0]
    tm = min(OUT_ROW_TILE, rows)
    args = [x2, o2, gpre, w_gate, w_o, gpost]
    in_specs = [_row_spec(tm, D_MODEL), _row_spec(tm, o2.shape[1]), _full_spec(gpre.shape),
                _full_spec(w_gate.shape), _full_spec(w_o.shape), _full_spec(gpost.shape)]
    out_specs = _row_spec(tm, D_MODEL)
    out_shape = jax.ShapeDtypeStruct((rows, D_MODEL), F32)
    if next_wq is not None:
        args += [next_gq, next_wq]
        in_specs += [_full_spec(next_gq.shape), _full_spec(next_wq.shape)]
        out_specs = [out_specs, _row_spec(tm, next_wq.shape[1])]
        out_shape = [out_shape, jax.ShapeDtypeStruct((rows, next_wq.shape[1]), BF16)]
    return pl.pallas_call(
        _out_kernel,
        grid=(rows // tm,),
        in_specs=in_specs,
        out_specs=out_specs,
        out_shape=out_shape,
        compiler_params=_row_params(),
        name="out_proj_q" if next_wq is not None else "out_proj",
    )(*args)


def _mla_attn_kernel(q_ref, k_ref, vt_ref, o_ref, s_scr, max_scr, m_scr, acc_scr, *, tq, tk, seq, heads):
    n_query_tiles = seq // tq
    key = lax.broadcasted_iota(jnp.int32, (CHUNK, tq), 0)
    qry = lax.broadcasted_iota(jnp.int32, (CHUNK, tq), 1)

    def load_queries(qi):
        r0 = pl.multiple_of(qi * tq, tq)
        return [q_ref[pl.ds(r0, tq), hd * LANES:(hd + 1) * LANES] for hd in range(heads)]

    def score_head(qs, j, slot, hd):
        c0 = pl.multiple_of(j * tk, tk)
        s = _dot_nt(k_ref[pl.ds(c0, tk), hd * LANES:(hd + 1) * LANES], qs[hd])
        s_scr[slot, hd] = s
        max_scr[slot, hd] = jnp.max(s, axis=0, keepdims=True)

    def score_chunk(slot, hd, c, diag_offset):
        blk = s_scr[slot, hd, c * CHUNK:(c + 1) * CHUNK, :]
        if diag_offset is None or (c + 1) * CHUNK - 1 <= diag_offset:
            return blk
        return jnp.where(key + (c * CHUNK - diag_offset) <= qry, blk, -jnp.inf)

    def tile(j, slot, prefetch, diag_offset=None):
        n_chunks = tk // CHUNK if diag_offset is None else (diag_offset + tq) // CHUNK
        pending = None
        for hd in range(heads):
            for ahead in range(PREFETCH_HEADS * hd, min(PREFETCH_HEADS * (hd + 1), heads)):
                prefetch(ahead)
            if diag_offset is None:
                mx = max_scr[slot, hd]
            else:
                mx = score_chunk(slot, hd, 0, diag_offset)
                for c in range(1, n_chunks):
                    mx = jnp.maximum(mx, score_chunk(slot, hd, c, diag_offset))
                mx = jnp.max(mx, axis=0, keepdims=True)
            m = m_scr[hd]
            m_new = jnp.maximum(m, mx)
            m_scr[hd] = m_new
            alpha = jnp.exp2(m - m_new)
            ps = [jnp.exp2(score_chunk(slot, hd, c, diag_offset) - m_new).astype(BF16)
                  for c in range(n_chunks)]
            pv = _dot(vt_ref[j, hd * A_VROWS:(hd + 1) * A_VROWS, :n_chunks * CHUNK], jnp.concatenate(ps, axis=0))
            if pending is not None:
                acc_scr[pending[0]] = pending[1] * acc_scr[pending[0]] + pending[2]
            pending = (hd, alpha, pv)
        acc_scr[pending[0]] = pending[1] * acc_scr[pending[0]] + pending[2]

    def q_body(qi, first_buffer):
        r0 = pl.multiple_of(qi * tq, tq)
        qs = load_queries(qi)
        diag = (qi * tq) // tk
        for hd in range(heads):
            m_scr[hd] = jnp.full((1, tq), -jnp.inf, F32)
            acc_scr[hd] = jnp.zeros((A_VROWS, tq), F32)

        def body(j, _):
            for slot in range(2):
                @pl.when((j + first_buffer) % 2 == slot)
                def _():
                    tile(j, slot, lambda hd: score_head(qs, j + 1, 1 - slot, hd))
            return 0

        lax.fori_loop(0, diag, body, 0)
        diag_buffer = (diag + first_buffer) % 2
        position = (r0 - diag * tk) // tq
        for slot in range(2):
            for pos in range(tk // tq):
                @pl.when(jnp.logical_and(diag_buffer == slot, position == pos))
                def _():
                    qs_next = load_queries(jnp.minimum(qi + 1, n_query_tiles - 1))
                    tile(diag, slot, lambda hd: score_head(qs_next, 0, 1 - slot, hd), pos * tq)
                    outs = [acc_scr[hd, :A_VDIM, :] / acc_scr[hd, A_VDIM:A_VDIM + 1, :] for hd in range(heads)]
                    o_ref[pl.ds(r0, tq), :] = jnp.concatenate(outs, axis=0).T
        return 1 - diag_buffer

    qs0 = load_queries(0)
    for hd in range(heads):
        score_head(qs0, 0, 0, hd)
    lax.fori_loop(0, n_query_tiles, q_body, jnp.int32(0))


def _sb_attn_kernel(q_ref, k_ref, vt_ref, later_ref, o_ref, z_scr, later_scr, acc_scr, *, tq, tk, seq, heads):
    blk = later_ref.shape[1]
    n_query_tiles = seq // tq
    lane = lax.broadcasted_iota(jnp.int32, (tq, LANES), 1)
    key = lax.broadcasted_iota(jnp.int32, (CHUNK, tq), 0)
    qry = lax.broadcasted_iota(jnp.int32, (CHUNK, tq), 1)

    def load_queries(qi):
        r0 = pl.multiple_of(qi * tq, tq)
        qs = []
        for hd in range(heads):
            pair = q_ref[pl.ds(r0, tq), (hd // 2) * LANES:(hd // 2 + 1) * LANES]
            mine = (lane < B_DIM) if hd % 2 == 0 else (lane >= B_DIM)
            qs.append(jnp.where(mine, pair, jnp.zeros_like(pair)))
        return qs

    def logits(qs, j, slot, hd):
        c0 = pl.multiple_of(j * tk, tk)
        k = k_ref[pl.ds(c0, tk), (hd // 2) * LANES:(hd // 2 + 1) * LANES]
        z_scr[slot, hd] = _dot_nt(k, qs[hd])

    def q_body(qi, _):
        r0 = pl.multiple_of(qi * tq, tq)
        qs = load_queries(qi)
        diag = (qi * tq) // tk

        def cumulative_sums(slot, hd, b, strict_limit):
            parts = []
            for c in range(blk // CHUNK):
                rows = slice(b * blk + c * CHUNK, b * blk + (c + 1) * CHUNK)
                z = z_scr[slot, hd, rows, :]
                neg_lse = jnp.log(1.0 + jnp.exp2(-jnp.abs(z))) * (-LOG2E)
                log_rest = neg_lse - jnp.maximum(z, 0.0)
                z_scr[slot, hd, rows, :] = log_rest + z
                if strict_limit is not None:
                    log_rest = jnp.where(key + (b * blk + c * CHUNK) < strict_limit, log_rest, 0.0)
                parts.append(log_rest.astype(BF16))
            return _dot(later_ref[...], jnp.concatenate(parts, axis=0))

        def weigh_values(j, slot, hd, b, sums, later, strict_limit):
            weights = []
            for c in range(blk // CHUNK):
                rows = slice(b * blk + c * CHUNK, b * blk + (c + 1) * CHUNK)
                a = jnp.exp2(z_scr[slot, hd, rows, :] + sums[c * CHUNK:(c + 1) * CHUNK, :] + later)
                if strict_limit is not None:
                    a = jnp.where(key + (b * blk + c * CHUNK) < strict_limit, a, 0.0)
                weights.append(a.astype(BF16))
            vt = vt_ref[j, hd * B_DIM:(hd + 1) * B_DIM, b * blk:(b + 1) * blk]
            return later + sums[blk:blk + 1], _dot(vt, jnp.concatenate(weights, axis=0))

        def tile(j, slot, strict_limit=None, ahead=None, recycle=None):
            blocks = list(reversed(range(tk // blk)))
            pending = None
            for hd in range(heads + 1):
                sums = [cumulative_sums(slot, hd, b, strict_limit) for b in blocks] if hd < heads else None
                if ahead is not None and hd < heads:
                    logits(qs, ahead[0], ahead[1], hd)
                if pending is not None:
                    later = later_scr[hd - 1]
                    for b, block_sums in zip(blocks, pending):
                        later, weighted = weigh_values(j, slot, hd - 1, b, block_sums, later, strict_limit)
                        acc_scr[hd - 1] += weighted
                    later_scr[hd - 1] = later
                    if recycle is not None:
                        logits(recycle[0], recycle[1], slot, hd - 1)
                pending = sums

        def any_weight_left():
            worst = later_scr[0]
            for hd in range(1, heads):
                worst = jnp.maximum(worst, later_scr[hd])
            return (jnp.max(worst) > F32_UNDERFLOW_LOG2).astype(jnp.int32)

        def more(state):
            n, live = state
            return jnp.logical_and(n <= diag, live > 0)

        def body(state):
            n, _ = state
            j = diag - n

            @pl.when(n >= 2)
            def _():
                for hd in range(heads):
                    logits(qs, j, 1, hd)

            tile(j, 1)
            return n + 1, any_weight_left()

        for hd in range(heads):
            later_scr[hd] = jnp.zeros((1, tq), F32)
            acc_scr[hd] = jnp.zeros((B_DIM, tq), F32)
        qi_next = jnp.minimum(qi + 1, n_query_tiles - 1)
        tile(diag, 0, qry + (r0 - diag * tk), ahead=(jnp.maximum(diag - 1, 0), 1),
             recycle=(load_queries(qi_next), (qi_next * tq) // tk))
        lax.while_loop(more, body, (jnp.int32(1), any_weight_left()))
        o_ref[pl.ds(r0, tq), :] = jnp.concatenate([acc_scr[hd] for hd in range(heads)], axis=0).T
        return 0

    qs0 = load_queries(0)
    for hd in range(heads):
        logits(qs0, 0, 0, hd)
    lax.fori_loop(0, n_query_tiles, q_body, 0)


def _attn_params():
    return pltpu.CompilerParams(dimension_semantics=("parallel", "parallel"), vmem_limit_bytes=VMEM_LIMIT)


def _mla_attn(q, k, vt, tq):
    batch, seq, _ = q.shape
    tk = vt.shape[-1]
    heads = MLA_HEADS_PER_STEP
    qk_spec = pl.BlockSpec((None, seq, heads * LANES), lambda b, g: (b, 0, g))
    return pl.pallas_call(
        functools.partial(_mla_attn_kernel, tq=tq, tk=tk, seq=seq, heads=heads),
        grid=(batch, A_HEADS // heads),
        in_specs=[qk_spec, qk_spec,
                  pl.BlockSpec((None, seq // tk, heads * A_VROWS, tk), lambda b, g: (b, 0, g, 0))],
        out_specs=pl.BlockSpec((None, seq, heads * A_VDIM), lambda b, g: (b, 0, g)),
        out_shape=jax.ShapeDtypeStruct((batch, seq, A_HEADS * A_VDIM), F32),
        scratch_shapes=[pltpu.VMEM((2, heads, tk, tq), F32), pltpu.VMEM((2, heads, 1, tq), F32),
                        pltpu.VMEM((heads, 1, tq), F32), pltpu.VMEM((heads, A_VROWS, tq), F32)],
        compiler_params=_attn_params(),
        name="mla_attn",
    )(q, k, vt)


def _sb_attn(q, k, vt, later2, tq):
    batch, seq, _ = q.shape
    tk = vt.shape[-1]
    heads = SB_HEADS_PER_STEP
    spec = pl.BlockSpec((None, seq, heads * B_DIM), lambda b, g: (b, 0, g))
    return pl.pallas_call(
        functools.partial(_sb_attn_kernel, tq=tq, tk=tk, seq=seq, heads=heads),
        grid=(batch, B_HEADS // heads),
        in_specs=[spec, spec, pl.BlockSpec((None, seq // tk, heads * B_DIM, tk), lambda b, g: (b, 0, g, 0)),
                  pl.BlockSpec(later2.shape, lambda b, g: (0, 0))],
        out_specs=spec,
        out_shape=jax.ShapeDtypeStruct((batch, seq, B_HEADS * B_DIM), F32),
        scratch_shapes=[pltpu.VMEM((2, heads, tk, tq), F32), pltpu.VMEM((heads, 1, tq), F32),
                        pltpu.VMEM((heads, B_DIM, tq), F32)],
        compiler_params=_attn_params(),
        name="sb_attn",
    )(q, k, vt, later2)


def _rope_tables(seq):
    pos = jnp.arange(seq, dtype=F32)
    inv = 1.0 / (ROPE_THETA ** (jnp.arange(0, A_ROPE, 2, dtype=F32) / A_ROPE))
    ang = pos[:, None] * inv[None, :]
    cos, sin = jnp.cos(ang), jnp.sin(ang)
    ones = jnp.ones((seq, A_NOPE), F32)
    zeros_n = jnp.zeros((seq, A_NOPE), F32)
    zeros_p = jnp.zeros((seq, LANES - A_NOPE - A_ROPE), F32)
    cos_t = jnp.concatenate([ones, cos, cos, zeros_p], axis=1)
    sin_t = jnp.concatenate([zeros_n, sin, sin, zeros_p], axis=1)
    return cos_t, sin_t


def _rot_half_cols(w):
    half = A_ROPE // 2
    return jnp.concatenate([-w[..., half:], w[..., :half]], axis=-1)


def _mla_weights(w_in, w_uq, w_ukv):
    d = w_in.shape[0]
    pad = LANES - A_NOPE - A_ROPE
    c_q = w_in[:, :A_QLORA]
    c_kv = w_in[:, A_QLORA:A_QLORA + A_KVLORA]
    k_r = w_in[:, A_QLORA + A_KVLORA:A_QLORA + A_KVLORA + A_ROPE]
    w_gate = w_in[:, A_QLORA + A_KVLORA + A_ROPE:]
    assert pad == A_ROPE
    zn = jnp.zeros((d, A_NOPE), F32)
    w_in_p = jnp.concatenate([c_q, c_kv, zn, k_r, _rot_half_cols(k_r)], axis=1)

    wq = w_uq.reshape(A_QLORA, A_HEADS, A_NOPE + A_ROPE)
    nope, rope = wq[..., :A_NOPE], wq[..., A_NOPE:]
    w_q_p = jnp.concatenate([nope, rope, _rot_half_cols(rope)], axis=-1).reshape(A_QLORA, A_HEADS * LANES)

    wkv = w_ukv.reshape(A_KVLORA, A_HEADS, A_NOPE + A_VDIM)
    kn, vv = wkv[..., :A_NOPE], wkv[..., A_NOPE:]
    zk = jnp.zeros((A_KVLORA, A_HEADS, LANES - A_NOPE), F32)
    w_k_p = jnp.concatenate([kn, zk], axis=-1).reshape(A_KVLORA, A_HEADS * LANES)
    w_vt = vv.reshape(A_KVLORA, A_HEADS * A_VDIM).T
    return (w_in_p.astype(BF16), w_gate.astype(BF16), w_q_p.astype(BF16), w_k_p.astype(BF16),
            w_vt.astype(BF16))


def _later_key_matrix(blk):
    s = lax.broadcasted_iota(jnp.int32, (blk, blk), 0)
    j = lax.broadcasted_iota(jnp.int32, (blk, blk), 1)
    return jnp.concatenate([(j > s).astype(BF16), jnp.ones((BF16_SUBLANES, blk), BF16)], axis=0)


def kernel(x, a_norm_pre, a_w_in, a_q_norm, a_w_uq, a_kv_norm, a_w_ukv, a_w_o, a_norm_post,
           b_kv_norm, b_w_kv, b_norm_pre, b_w_in, b_w_o, b_norm_post):
    batch, seq, d = x.shape
    rows = batch * seq
    tq = min(ATTN_TQ, seq)
    tk = min(MLA_TK, seq)
    x2 = x.reshape(rows, d)
    cos_t, sin_t = _rope_tables(seq)
    n_a = a_w_in.shape[0]
    n_b = b_w_in.shape[0]

    for i in range(n_a):
        w_in_p, w_gate, w_q_p, w_k_p, w_vt = _mla_weights(a_w_in[i], a_w_uq[i], a_w_ukv[i])
        q, k, vt = _mla_proj(x2, a_norm_pre[i][None], w_in_p, a_q_norm[i][None], w_q_p,
                             a_kv_norm[i][None], w_k_p, w_vt, cos_t, sin_t, seq, tk)
        o = _mla_attn(q.reshape(batch, seq, -1), k.reshape(batch, seq, -1), vt, tq)
        x2 = _out_proj(x2, o.reshape(rows, -1), a_norm_pre[i][None], w_gate,
                       a_w_o[i].astype(BF16), a_norm_post[i][None])

    width = B_HEADS * B_DIM
    tk = min(SB_TK, seq)
    later2 = _later_key_matrix(min(SB_BLOCK, tk))
    q, k, vt = _sb_proj(x2, b_norm_pre[0][None], b_w_in[0][:, :width].astype(BF16), seq, tk, b_kv_norm[None],
                        b_w_kv[:, :width].astype(BF16), b_w_kv[:, width:].T.astype(BF16))
    k = k.reshape(batch, seq, width)
    for j in range(n_b):
        w_gate = b_w_in[j][:, width:].astype(BF16)
        o = _sb_attn(q.reshape(batch, seq, width), k, vt, later2, tq)
        if j + 1 < n_b:
            x2, q = _out_proj(x2, o.reshape(rows, width), b_norm_pre[j][None], w_gate, b_w_o[j].astype(BF16),
                              b_norm_post[j][None], b_norm_pre[j + 1][None],
                              b_w_in[j + 1][:, :width].astype(BF16))
        else:
            x2 = _out_proj(x2, o.reshape(rows, width), b_norm_pre[j][None], w_gate,
                           b_w_o[j].astype(BF16), b_norm_post[j][None])
    return x2.reshape(batch, seq, d)
```
